```python
import math
import jax, jax.numpy as jnp
from jax import lax
import numpy as np

D_MODEL = 1024
BATCH = 4
SEQ = 4096
DEPTH = 2

GRID_W = 64
EPS = 1e-6
HEAD_DIM = 64
GLA_HEADS = 4
GLA_DK = 32
GLA_DV = 64
GLA_WIDTH = GLA_HEADS * GLA_DV
GLA_QK = GLA_HEADS * GLA_DK
GLA_LOWRANK = 16
GLA_GATE_NORM = 16.0
GLA_CHUNK = 32
ATT_HEADS = 8
ATT_KV_HEADS = 2
ATT_GROUP = ATT_HEADS // ATT_KV_HEADS
ATT_WIDTH = ATT_HEADS * HEAD_DIM
ATT_KV = ATT_KV_HEADS * HEAD_DIM
ATT_BLOCK = 128
ROPE_THETA = 10000.0
ROPE_HALF = HEAD_DIM // 2
HY_WIDTH = D_MODEL - GLA_WIDTH - ATT_WIDTH
HY_EMB = 33
HY_BANDS = (HY_EMB - 1) // 2
HY_ORDER_DIM = 64
HY_INNER = 2
HY_TARGET = 1e-2
HY_FAST = 0.3
HY_SLOW = 1.5
HY_MIN_DECAY = math.log(HY_TARGET) / HY_SLOW
HY_MAX_DECAY = math.log(HY_TARGET) / HY_FAST
N_EXPERTS = 16
EC_FACTOR = 2
EXPERT_FF = 2816
SPLITS = (GLA_QK, GLA_QK, GLA_WIDTH, GLA_WIDTH, 2 * GLA_LOWRANK, ATT_WIDTH, ATT_KV, ATT_KV, 3 * HY_WIDTH)
D_IN_PROJ = sum(SPLITS)

kernel_name = 'hybrid_gla_gqa_hyena_ec_moe_encoder'


def rmsnorm(x, g):
    xf = x.astype(jnp.float32)
    y = xf * lax.rsqrt(jnp.mean(xf * xf, axis=-1, keepdims=True) + EPS)
    return (y * g.astype(jnp.float32)).astype(x.dtype)


def axial_rope_tables(L):
    n_rows = L // GRID_W
    row = jnp.repeat(jnp.arange(n_rows), GRID_W).astype(jnp.float32)
    col = jnp.tile(jnp.arange(GRID_W), n_rows).astype(jnp.float32)
    inv = 1.0 / (ROPE_THETA ** (jnp.arange(0, ROPE_HALF, 2, dtype=jnp.float32) / ROPE_HALF))
    ar = row[:, None] * inv[None]
    ac = col[:, None] * inv[None]
    ang = jnp.concatenate([ar, ar, ac, ac], axis=-1)
    return jnp.cos(ang), jnp.sin(ang)


def apply_axial_rope(x, cos, sin):
    xr = x.reshape(x.shape[:-1] + (2, 2, ROPE_HALF // 2))
    rot = jnp.stack([-xr[..., 1, :], xr[..., 0, :]], axis=-2).reshape(x.shape)
    return (x * cos[None, :, None, :] + rot * sin[None, :, None, :]).astype(x.dtype)


def gla_direction(q, k, v, log_a):
    B, H, L, dk = q.shape
    dv = v.shape[-1]
    n = L // GLA_CHUNK
    rs = lambda t: t.reshape(B, H, n, GLA_CHUNK, t.shape[-1])
    q, k, v, log_a = rs(q), rs(k), rs(v), rs(log_a)
    b = jnp.cumsum(log_a, axis=-2)
    lower = jnp.tril(jnp.ones((GLA_CHUNK, GLA_CHUNK), dtype=bool))
    diff = b[..., :, None, :] - b[..., None, :, :]
    decay = jnp.exp(jnp.where(lower[:, :, None], diff, -jnp.inf))
    scores = jnp.einsum('bhnid,bhnjd,bhnijd->bhnij', q, k, decay)
    o_intra = jnp.einsum('bhnij,bhnjv->bhniv', scores, v)
    b_last = b[..., -1:, :]
    q_dec = q * jnp.exp(b)
    k_dec = k * jnp.exp(b_last - b)
    a_last = jnp.exp(b_last[..., 0, :])

    def step(state, inp):
        qd, kd, vc, al = inp
        o = jnp.einsum('bhcd,bhdv->bhcv', qd, state)
        state = state * al[..., None] + jnp.einsum('bhcd,bhcv->bhdv', kd, vc)
        return state, o

    mv = lambda t: jnp.moveaxis(t, 2, 0)
    s0 = jnp.zeros((B, H, dk, dv), jnp.float32)
    _, o_inter = lax.scan(step, s0, (mv(q_dec), mv(k_dec), mv(v), mv(a_last)))
    o = o_intra + jnp.moveaxis(o_inter, 0, 2)
    return o.reshape(B, H, L, dv)


def gla_group(gq, gk, gv, gg, glr, w_dec, b_dec, norm_g):
    B, L, _ = gq.shape
    heads = lambda t, d: jnp.transpose(t.reshape(B, L, GLA_HEADS, d), (0, 2, 1, 3)).astype(jnp.float32)
    q = heads(gq, GLA_DK) * (GLA_DK ** -0.5)
    k = heads(gk, GLA_DK)
    v = heads(gv, GLA_DV)
    lr_f, lr_b = jnp.split(glr, 2, axis=-1)
    la_f = heads(jax.nn.log_sigmoid((lr_f @ w_dec[0] + b_dec[0]).astype(jnp.float32)) / GLA_GATE_NORM, GLA_DK)
    la_b = heads(jax.nn.log_sigmoid((lr_b @ w_dec[1] + b_dec[1]).astype(jnp.float32)) / GLA_GATE_NORM, GLA_DK)
    fl = lambda t: jnp.flip(t, axis=2)
    o = gla_direction(q, k, v, la_f) + fl(gla_direction(fl(q), fl(k), fl(v), fl(la_b)))
    o = jnp.transpose(o, (0, 2, 1, 3))
    o = rmsnorm(o, norm_g) * jax.nn.silu(gg.reshape(B, L, GLA_HEADS, GLA_DV).astype(jnp.float32))
    return o.reshape(B, L, GLA_WIDTH).astype(gq.dtype)


def attention_group(aq, ak, av, q_norm_g, k_norm_g, cos, sin):
    B, L, _ = aq.shape
    q = apply_axial_rope(rmsnorm(aq.reshape(B, L, ATT_HEADS, HEAD_DIM), q_norm_g), cos, sin)
    k = apply_axial_rope(rmsnorm(ak.reshape(B, L, ATT_KV_HEADS, HEAD_DIM), k_norm_g), cos, sin)
    v = av.reshape(B, L, ATT_KV_HEADS, HEAD_DIM)
    kf = k.astype(jnp.float32)
    vf = v.astype(jnp.float32)
    nb = L // ATT_BLOCK
    qb = jnp.moveaxis(q.reshape(B, nb, ATT_BLOCK, ATT_KV_HEADS, ATT_GROUP, HEAD_DIM), 1, 0)
    scale = HEAD_DIM ** -0.5

    def block(qc):
        s = jnp.einsum('bqkgd,bskd->bkgqs', qc.astype(jnp.float32), kf) * scale
        p = jax.nn.softmax(s, axis=-1)
        return jnp.einsum('bkgqs,bskd->bqkgd', p, vf)

    o = lax.map(block, qb)
    return jnp.moveaxis(o, 0, 1).reshape(B, L, ATT_WIDTH).astype(aq.dtype)


def short_conv3(u, w, b):
    up = jnp.pad(u, ((0, 0), (1, 1), (0, 0)))
    return up[:, :-2] * w[0] + up[:, 1:-1] * w[1] + up[:, 2:] * w[2] + b


def hyena_filters(L, w1, b1, w2, b2, w3, freq):
    pos = jnp.arange(L, dtype=jnp.float32)
    t = pos / (L - 1)
    f = jnp.linspace(1e-4, HY_BANDS - 1, HY_BANDS, dtype=jnp.float32)
    ang = (2.0 * math.pi * pos / L)[:, None] * f[None]
    z = jnp.concatenate([t[:, None], jnp.cos(ang), -jnp.sin(ang)], axis=-1)
    fr = freq.astype(jnp.float32)
    h = jnp.sin(fr * (z @ w1.astype(jnp.float32) + b1.astype(jnp.float32)))
    for i in range(HY_INNER):
        h = jnp.sin(fr * (h @ w2[i].astype(jnp.float32) + b2[i].astype(jnp.float32)))
    h = h @ w3.astype(jnp.float32)
    deltas = jnp.abs(jnp.linspace(HY_MIN_DECAY, HY_MAX_DECAY, HY_WIDTH, dtype=jnp.float32))
    window = jnp.exp(-t[:, None] * deltas[None])
    h = h * jnp.concatenate([window, window], axis=-1)
    h_f, h_b = h[:, :HY_WIDTH], h[:, HY_WIDTH:]
    return jnp.concatenate([h_f, jnp.zeros((1, HY_WIDTH), jnp.float32), h_b[1:][::-1]], axis=0)


def long_conv(z, h_full):
    L = z.shape[1]
    Z = jnp.fft.rfft(z.astype(jnp.float32), n=2 * L, axis=1)
    Hf = jnp.fft.rfft(h_full, n=2 * L, axis=0)
    return jnp.fft.irfft(Z * Hf[None], n=2 * L, axis=1)[:, :L]


def hyena_group(hy, conv_w, conv_b, w1, b1, w2, b2, w3, freq, skip):
    L = hy.shape[1]
    u = short_conv3(hy, conv_w, conv_b)
    x0, x1, v = jnp.split(u, 3, axis=-1)
    z = (v * x1).astype(jnp.float32)
    z = long_conv(z, hyena_filters(L, w1, b1, w2, b2, w3, freq)) + z * skip.astype(jnp.float32)
    return (x0.astype(jnp.float32) * z).astype(hy.dtype)


def expert_choice_ffn(h, router_w, w_gate, w_up, w_down):
    B, T, D = h.shape
    cap = EC_FACTOR * T // N_EXPERTS
    aff = jax.nn.softmax((h @ router_w).astype(jnp.float32), axis=-1)
    gates, idx = lax.top_k(jnp.transpose(aff, (0, 2, 1)), cap)
    xg = jax.vmap(lambda hb, ib: hb[ib])(h, idx)
    a = jnp.einsum('becd,edf->becf', xg, w_gate)
    u = jnp.einsum('becd,edf->becf', xg, w_up)
    y = jnp.einsum('becf,efd->becd', jax.nn.silu(a) * u, w_down)
    y = y * gates[..., None].astype(y.dtype)
    return jax.vmap(lambda ib, yb: jnp.zeros((T, D), yb.dtype).at[ib.reshape(-1)].add(yb.reshape(-1, D)))(idx, y)


def setup_inputs(seed: int = 0) -> dict:
    key = jax.random.key(seed)
    ks = jax.random.split(key, 30)
    nrm = lambda k, shape, s: jax.random.normal(k, shape, jnp.float32) * s
    gain = lambda k, shape: 1.0 + 0.05 * jax.random.normal(k, shape, jnp.float32)
    D = D_MODEL
    return {
        'x': nrm(ks[0], (BATCH, SEQ, D), 1.0),
        'c': nrm(ks[1], (BATCH, D), 1.0),
        'ada_w': nrm(ks[2], (DEPTH, D, 6 * D), 0.5 * D ** -0.5),
        'ada_b': nrm(ks[3], (DEPTH, 6 * D), 0.02),
        'mix_pre_g': gain(ks[4], (DEPTH, D)),
        'mix_post_g': gain(ks[5], (DEPTH, D)),
        'w_in': nrm(ks[6], (DEPTH, D, D_IN_PROJ), D ** -0.5),
        'gla_w_dec': nrm(ks[7], (DEPTH, 2, GLA_LOWRANK, GLA_QK), GLA_LOWRANK ** -0.5),
        'gla_b_dec': nrm(ks[8], (DEPTH, 2, GLA_QK), 0.1),
        'gla_norm_g': gain(ks[9], (DEPTH, GLA_DV)),
        'q_norm_g': gain(ks[10], (DEPTH, HEAD_DIM)),
        'k_norm_g': gain(ks[11], (DEPTH, HEAD_DIM)),
        'hy_conv_w': nrm(ks[12], (DEPTH, 3, 3 * HY_WIDTH), 3 ** -0.5),
        'hy_conv_b': nrm(ks[13], (DEPTH, 3 * HY_WIDTH), 0.02),
        'hy_w1': nrm(ks[14], (DEPTH, HY_EMB, HY_ORDER_DIM), HY_EMB ** -0.5),
        'hy_b1': nrm(ks[15], (DEPTH, HY_ORDER_DIM), 0.1),
        'hy_w2': nrm(ks[16], (DEPTH, HY_INNER, HY_ORDER_DIM, HY_ORDER_DIM), HY_ORDER_DIM ** -0.5),
        'hy_b2': nrm(ks[17], (DEPTH, HY_INNER, HY_ORDER_DIM), 0.1),
        'hy_w3': nrm(ks[18], (DEPTH, HY_ORDER_DIM, 2 * HY_WIDTH), 0.005),
        'hy_freq': gain(ks[19], (DEPTH, HY_ORDER_DIM)),
        'hy_skip': nrm(ks[20], (DEPTH, HY_WIDTH), 1.0),
        'w_out': nrm(ks[21], (DEPTH, D, D), D ** -0.5),
        'ffn_pre_g': gain(ks[22], (DEPTH, D)),
        'ffn_post_g': gain(ks[23], (DEPTH, D)),
        'router_w': nrm(ks[24], (DEPTH, D, N_EXPERTS), D ** -0.5),
        'exp_w_gate': nrm(ks[25], (DEPTH, N_EXPERTS, D, EXPERT_FF), D ** -0.5),
        'exp_w_up': nrm(ks[26], (DEPTH, N_EXPERTS, D, EXPERT_FF), D ** -0.5),
        'exp_w_down': nrm(ks[27], (DEPTH, N_EXPERTS, EXPERT_FF, D), EXPERT_FF ** -0.5),
    }


def reference(x, c, ada_w, ada_b, mix_pre_g, mix_post_g, w_in, gla_w_dec, gla_b_dec, gla_norm_g,
              q_norm_g, k_norm_g, hy_conv_w, hy_conv_b, hy_w1, hy_b1, hy_w2, hy_b2, hy_w3, hy_freq,
              hy_skip, w_out, ffn_pre_g, ffn_post_g, router_w, exp_w_gate, exp_w_up, exp_w_down):
    L = x.shape[1]
    cos, sin = axial_rope_tables(L)
    cond = jax.nn.silu(c)
    split_at = np.cumsum(SPLITS)[:-1].tolist()
    for l in range(DEPTH):
        mod = (cond @ ada_w[l] + ada_b[l])[:, None, :]
        sh_m, sc_m, g_m, sh_f, sc_f, g_f = jnp.split(mod, 6, axis=-1)
        h = rmsnorm(x, mix_pre_g[l]) * (1.0 + sc_m) + sh_m
        proj = h @ w_in[l]
        gq, gk, gv, gg, glr, aq, ak, av, hy = jnp.split(proj, split_at, axis=-1)
        o_gla = gla_group(gq, gk, gv, gg, glr, gla_w_dec[l], gla_b_dec[l], gla_norm_g[l])
        o_att = attention_group(aq, ak, av, q_norm_g[l], k_norm_g[l], cos, sin)
        o_hy = hyena_group(hy, hy_conv_w[l], hy_conv_b[l], hy_w1[l], hy_b1[l], hy_w2[l], hy_b2[l],
                           hy_w3[l], hy_freq[l], hy_skip[l])
        y = jnp.concatenate([o_gla, o_att, o_hy], axis=-1) @ w_out[l]
        x = x + g_m * rmsnorm(y, mix_post_g[l])
        h = rmsnorm(x, ffn_pre_g[l]) * (1.0 + sc_f) + sh_f
        y = expert_choice_ffn(h, router_w[l], exp_w_gate[l], exp_w_up[l], exp_w_down[l])
        x = x + g_f * rmsnorm(y, ffn_post_g[l])
    return x
```

```python
import functools
import math

import jax
import jax.numpy as jnp
import numpy as np
from jax import lax
from jax.experimental import pallas as pl
from jax.experimental.pallas import tpu as pltpu

F32 = jnp.float32
BF16 = jnp.bfloat16
HI = lax.Precision.HIGHEST
SDS = jax.ShapeDtypeStruct

EPS = 1e-6
GRID_W = 64
HEAD_DIM = 64
GLA_HEADS = 4
GLA_DK = 32
GLA_DV = 64
GLA_QK = GLA_HEADS * GLA_DK
GLA_WIDTH = GLA_HEADS * GLA_DV
GLA_LOWRANK = 16
GLA_GATE_NORM = 16.0
GLA_BLOCK = 16
ATT_HEADS = 8
ATT_KV_HEADS = 2
ATT_GROUP = ATT_HEADS // ATT_KV_HEADS
ATT_WIDTH = ATT_HEADS * HEAD_DIM
ATT_KV = ATT_KV_HEADS * HEAD_DIM
ATT_TK = 256
ATT_ONES_ROWS = 16
ROPE_THETA = 10000.0
ROPE_HALF = HEAD_DIM // 2
HY_EMB = 33
HY_BANDS = (HY_EMB - 1) // 2
HY_TARGET = 1e-2
HY_FAST = 0.3
HY_SLOW = 1.5
HY_MIN_DECAY = math.log(HY_TARGET) / HY_SLOW
HY_MAX_DECAY = math.log(HY_TARGET) / HY_FAST
HY_BLOCK = 128
N_EXPERTS = 16
EC_FACTOR = 2
ROUTE_BISECTIONS = 160
LANES = 128
VMEM_LIMIT = 56 * 1024 * 1024


def _params(*sem):
    return pltpu.CompilerParams(dimension_semantics=sem, vmem_limit_bytes=VMEM_LIMIT)


def _silu(x):
    return x * (1.0 / (1.0 + jnp.exp(-x)))


def _rms(x):
    return x * lax.rsqrt(jnp.mean(x * x, axis=-1, keepdims=True) + EPS)


def _ada_kernel(c_ref, w_ref, b_ref, o_ref):
    cond = _silu(c_ref[...])
    o_ref[0] = jnp.dot(cond, w_ref[0], precision=HI, preferred_element_type=F32) + b_ref[0]


def _ada_call(c, ada_w, ada_b):
    depth, d, n6 = ada_w.shape
    b = c.shape[0]
    tn = 1024
    return pl.pallas_call(
        _ada_kernel,
        grid=(depth, n6 // tn),
        in_specs=[
            pl.BlockSpec((b, d), lambda l, j: (0, 0)),
            pl.BlockSpec((1, d, tn), lambda l, j: (l, 0, j)),
            pl.BlockSpec((1, 1, tn), lambda l, j: (l, 0, j)),
        ],
        out_specs=pl.BlockSpec((1, b, tn), lambda l, j: (l, 0, j)),
        out_shape=SDS((depth, b, n6), F32),
        compiler_params=_params("arbitrary", "arbitrary"),
        name="ada_mod",
    )(c, ada_w, ada_b.reshape(depth, 1, n6))


_C_GQK = (0, 256)
_C_GV = (256, 512)
_C_GG = (512, 768)
_C_GLR = (768, 896)
_C_AQ = (896, 1408)
_C_AK = (1408, 1536)
_C_AV = (1536, 1664)
_C_HY = (1664, 2432)
_NPACK = 2432


def _head_mean_matrix():
    r = lax.broadcasted_iota(jnp.int32, (LANES, LANES), 0) // HEAD_DIM
    c = lax.broadcasted_iota(jnp.int32, (LANES, LANES), 1) // HEAD_DIM
    return jnp.where(r == c, 1.0 / HEAD_DIM, 0.0).astype(BF16)


def _norm_rope(a, gain, cos, sin_signed, gmat, first_half):
    ms = jnp.dot((a * a).astype(BF16), gmat, preferred_element_type=F32)
    xn = a * lax.rsqrt(ms + EPS) * gain
    rot = jnp.where(first_half, pltpu.roll(xn, LANES - ROPE_HALF // 2, 1), pltpu.roll(xn, ROPE_HALF // 2, 1))
    return xn * cos + rot * sin_signed


def _inproj_kernel(x_ref, mod_ref, g_ref, w_ref, cos_ref, sin_ref, qg_ref, kg_ref,
                   gqk_ref, gv_ref, gg_ref, glr_ref, q_ref, k_ref, vt_ref, hy_ref, *, tk):
    x = x_ref[0]
    sh = mod_ref[0, 0:1, :]
    sc = mod_ref[0, 1:2, :]
    h = (_rms(x) * g_ref[...] * (1.0 + sc) + sh).astype(BF16)

    def seg(c):
        return jnp.dot(h, w_ref[:, c[0]:c[1]], preferred_element_type=F32)

    gqk_ref[0] = seg(_C_GQK)
    gv_ref[0] = seg(_C_GV)
    gg_ref[0] = seg(_C_GG)
    glr_ref[0] = seg(_C_GLR)
    hy_ref[0] = seg(_C_HY)
    av = seg(_C_AV)
    for ci in range(av.shape[0] // tk):
        vt_ref[0, ci] = av[ci * tk:(ci + 1) * tk, :].T.astype(BF16)

    gmat = _head_mean_matrix()
    cos = cos_ref[...]
    sin_signed = sin_ref[...]
    lane = lax.broadcasted_iota(jnp.int32, cos.shape, 1)
    first_half = (lane % ROPE_HALF) < (ROPE_HALF // 2)
    aq = seg(_C_AQ)
    scale = HEAD_DIM ** -0.5
    for j in range(ATT_WIDTH // LANES):
        qj = _norm_rope(aq[:, j * LANES:(j + 1) * LANES], qg_ref[...], cos, sin_signed, gmat, first_half)
        q_ref[0, :, j * LANES:(j + 1) * LANES] = (qj * scale).astype(BF16)
    k_ref[0] = _norm_rope(seg(_C_AK), kg_ref[...], cos, sin_signed, gmat, first_half).astype(BF16)


def _inproj_call(x, mod_l, g, w_pack, cos2, sin2, qg2, kg2):
    b, l, d = x.shape
    tm = min(512, l)
    tk = min(ATT_TK, l)
    row = lambda bi, i: (bi, i, 0)
    full2 = lambda bi, i: (0, 0)
    rows_out = [
        (SDS((b, l, 256), F32), 256), (SDS((b, l, 256), F32), 256), (SDS((b, l, 256), F32), 256),
        (SDS((b, l, 128), F32), 128), (SDS((b, l, ATT_WIDTH), BF16), ATT_WIDTH),
        (SDS((b, l, ATT_KV), BF16), ATT_KV),
    ]
    out_specs = [pl.BlockSpec((1, tm, w), row) for _, w in rows_out]
    out_shape = [s for s, _ in rows_out]
    out_specs.append(pl.BlockSpec((1, tm // tk, ATT_KV, tk), lambda bi, i: (bi, i, 0, 0)))
    out_shape.append(SDS((b, l // tk, ATT_KV, tk), BF16))
    out_specs.append(pl.BlockSpec((1, tm, 768), row))
    out_shape.append(SDS((b, l, 768), F32))
    return pl.pallas_call(
        functools.partial(_inproj_kernel, tk=tk),
        grid=(b, l // tm),
        in_specs=[
            pl.BlockSpec((1, tm, d), row),
            pl.BlockSpec((1, 6, d), lambda bi, i: (bi, 0, 0)),
            pl.BlockSpec((1, d), full2),
            pl.BlockSpec((d, _NPACK), full2),
            pl.BlockSpec((tm, LANES), lambda bi, i: (i, 0)),
            pl.BlockSpec((tm, LANES), lambda bi, i: (i, 0)),
            pl.BlockSpec((1, LANES), full2),
            pl.BlockSpec((1, LANES), full2),
        ],
        out_specs=out_specs,
        out_shape=out_shape,
        compiler_params=_params("parallel", "parallel"),
        name="in_proj",
    )(x, mod_l, g, w_pack, cos2, sin2, qg2, kg2)


def _attn_kernel(q_ref, k_ref, vt_ref, o_ref, acc_ref, s_ref, p_ref, *, tq):
    nchunk = vt_ref.shape[1]
    tk = vt_ref.shape[3]
    lane = lax.broadcasted_iota(jnp.int32, (tq, LANES), 1)
    lower = lane < HEAD_DIM
    ones_rows = ATT_ONES_ROWS
    vrows = HEAD_DIM + ones_rows
    row = lax.broadcasted_iota(jnp.int32, (vrows, tk), 0)
    q = q_ref[0].astype(F32)
    streams = []
    qts = []
    for j in range(ATT_KV_HEADS):
        keep = lower if j == 0 else jnp.logical_not(lower)
        for hp in range(ATT_GROUP // 2):
            cb = j * (ATT_GROUP // 2) + hp
            qc = q[:, cb * LANES:(cb + 1) * LANES]
            qsw = pltpu.roll(qc, HEAD_DIM, 1)
            first, second = (qc, qsw) if j == 0 else (qsw, qc)
            cols = [jnp.where(keep, first, 0.0).T, jnp.where(keep, second, 0.0).T]
            qts.append(jnp.concatenate(cols, axis=1).astype(BF16))
            streams.append((j, cb))
    qt_all = jnp.concatenate(qts, axis=1)
    half = ATT_GROUP * tq
    acc_ref[...] = jnp.zeros(acc_ref.shape, F32)
    p_ref[1] = jnp.zeros(p_ref.shape[1:], BF16)
    ones = jnp.ones((vrows, tk), BF16)

    def scores(c):
        kc = k_ref[0, pl.ds(pl.multiple_of(c * tk, tk), tk), :]
        return jnp.dot(kc, qt_all, preferred_element_type=F32)

    def add_values(c, slot, alpha):
        vt = vt_ref[0, c]
        for j in range(ATT_KV_HEADS):
            r0 = j * (LANES - vrows)
            vm = jnp.where((row < HEAD_DIM) if j == 0 else (row >= ones_rows), vt[r0:r0 + vrows, :], ones)
            pj = p_ref[slot, :, j * half:(j + 1) * half]
            acc_ref[j] = acc_ref[j] * alpha[:, j * half:(j + 1) * half] + jnp.dot(vm, pj, preferred_element_type=F32)

    s_ref[0] = scores(0)

    def step(c, slot, carry):
        m_prev, alpha_prev = carry
        s_ref[1 - slot] = scores(jnp.minimum(c + 1, nchunk - 1))
        add_values(jnp.maximum(c - 1, 0), 1 - slot, alpha_prev)
        st = s_ref[slot]
        m_new = jnp.maximum(m_prev, jnp.max(st, axis=0, keepdims=True))
        p_ref[slot] = jnp.exp((st - m_new).astype(BF16))
        return m_new, jnp.exp(m_prev - m_new)

    def body(i, carry):
        return step(2 * i + 1, 1, step(2 * i, 0, carry))

    assert nchunk % 2 == 0
    init = (jnp.full((1, 2 * half), -jnp.inf, F32), jnp.ones((1, 2 * half), F32))
    _, alpha = lax.fori_loop(0, nchunk // 2, body, init)
    add_values(nchunk - 1, 1, alpha)

    for j, cb in streams:
        hp = cb % (ATT_GROUP // 2)
        acc = acc_ref[j][:, hp * 2 * tq:(hp + 1) * 2 * tq]
        d0 = j * ones_rows
        denom = acc[(1 - j) * HEAD_DIM:(1 - j) * HEAD_DIM + 1, :]
        data = acc[d0:d0 + HEAD_DIM, :] / denom
        blk = jnp.concatenate([data[:, :tq], data[:, tq:]], axis=0)
        o_ref[0, :, cb * LANES:(cb + 1) * LANES] = blk.T.astype(BF16)


def _attn_call(q, k, vt):
    b, l, _ = q.shape
    tq = min(256, l)
    nchunk, tk = vt.shape[1], vt.shape[3]
    nq = ATT_HEADS * tq
    return pl.pallas_call(
        functools.partial(_attn_kernel, tq=tq),
        grid=(b, l // tq),
        in_specs=[
            pl.BlockSpec((1, tq, ATT_WIDTH), lambda bi, i: (bi, i, 0)),
            pl.BlockSpec((1, l, ATT_KV), lambda bi, i: (bi, 0, 0)),
            pl.BlockSpec((1, nchunk, ATT_KV, tk), lambda bi, i: (bi, 0, 0, 0)),
        ],
        out_specs=pl.BlockSpec((1, tq, ATT_WIDTH), lambda bi, i: (bi, i, 0)),
        out_shape=SDS((b, l, ATT_WIDTH), BF16),
        scratch_shapes=[pltpu.VMEM((ATT_KV_HEADS, HEAD_DIM + ATT_ONES_ROWS, nq // ATT_KV_HEADS), F32),
                        pltpu.VMEM((2, tk, nq), F32), pltpu.VMEM((2, tk, nq), BF16)],
        compiler_params=_params("parallel", "parallel"),
        name="gqa_attention",
    )(q, k, vt)


def _log_sigmoid(x):
    return jnp.minimum(x, 0.0) - jnp.log(1.0 + jnp.exp(-jnp.abs(x)))


def _gla_kernel(qk_ref, v_ref, gg_ref, lr_ref, wdec_ref, bdec_ref, ng_ref, o_ref,
                qs_ref, ks_ref, a_ref, od_ref, st_ref, *, rt):
    seq = qk_ref.shape[1]
    cb = GLA_BLOCK
    rowmod = lax.broadcasted_iota(jnp.int32, (rt, GLA_QK), 0) % cb
    rowmod_v = lax.broadcasted_iota(jnp.int32, (rt, GLA_WIDTH), 0) % cb
    hd = lax.broadcasted_iota(jnp.int32, (GLA_QK, GLA_WIDTH), 0) // GLA_DK
    hc = lax.broadcasted_iota(jnp.int32, (GLA_QK, GLA_WIDTH), 1) // GLA_DV
    expand = jnp.where(hd == hc, 1.0, 0.0).astype(BF16)
    scale = GLA_DK ** -0.5

    def prep(t, _):
        r0 = pl.multiple_of(t * rt, rt)
        rows = pl.ds(r0, rt)
        q = qk_ref[0, rows, 0:GLA_QK] * scale
        k = qk_ref[0, rows, GLA_QK:2 * GLA_QK]
        v = v_ref[0, rows, :]
        lr = lr_ref[0, rows, :]
        od = jnp.zeros((rt, GLA_WIDTH), F32)
        for direction in range(2):
            la = _log_sigmoid(jnp.dot(lr, wdec_ref[direction], precision=HI, preferred_element_type=F32)
                              + bdec_ref[direction]) * (1.0 / GLA_GATE_NORM)
            pre = la
            suf = la
            s = 1
            while s < cb:
                pre = pre + jnp.where(rowmod >= s, pltpu.roll(pre, s, 0), 0.0)
                suf = suf + jnp.where(rowmod < cb - s, pltpu.roll(suf, rt - s, 0), 0.0)
                s *= 2
            tot = pre + suf - la
            cum = pre if direction == 0 else suf
            rem = (suf if direction == 0 else pre) - la
            qs_ref[direction, rows, :] = q * jnp.exp(cum)
            ks_ref[direction, rows, :] = k * jnp.exp(rem)
            a_ref[direction, rows, :] = jnp.exp(tot)
            for delta in range(cb):
                if direction == 0:
                    sh = delta
                    valid = rowmod_v >= delta
                else:
                    sh = (rt - delta) % rt
                    valid = rowmod_v < cb - delta
                if delta == 0:
                    a_mat = q * k
                    vj = v
                else:
                    kj = pltpu.roll(k, sh, 0)
                    cj = pltpu.roll(cum, sh, 0)
                    vj = pltpu.roll(v, sh, 0)
                    a_mat = q * kj * jnp.exp(jnp.minimum(cum - cj, 0.0))
                se = jnp.dot(a_mat.astype(BF16), expand, preferred_element_type=F32)
                od = od + jnp.where(valid, se * vj, 0.0)
        od_ref[rows, :] = od
        return 0

    lax.fori_loop(0, seq // rt, prep, 0)

    sr = lax.broadcasted_iota(jnp.int32, (GLA_WIDTH, GLA_QK), 0) // GLA_DV
    scol = lax.broadcasted_iota(jnp.int32, (GLA_WIDTH, GLA_QK), 1) // GLA_DK
    same_head = sr == scol
    st_ref[...] = jnp.zeros(st_ref.shape, F32)
    nblk = seq // cb

    def step(i, _):
        for direction in range(2):
            blk = i if direction == 0 else nblk - 1 - i
            rows = pl.ds(pl.multiple_of(blk * cb, cb), cb)
            qb = qs_ref[direction, rows, :].astype(BF16)
            kb = ks_ref[direction, rows, :].astype(BF16)
            vb = v_ref[0, rows, :].astype(BF16)
            ab = a_ref[direction, rows, :][0:1, :]
            st = st_ref[direction]
            od_ref[rows, :] += lax.dot_general(qb, st.astype(BF16), (((1,), (1,)), ((), ())),
                                               preferred_element_type=F32)
            kv = lax.dot_general(vb, kb, (((0,), (0,)), ((), ())), preferred_element_type=F32)
            st_ref[direction] = st * ab + jnp.where(same_head, kv, 0.0)
        return 0

    lax.fori_loop(0, nblk, step, 0)

    gr = lax.broadcasted_iota(jnp.int32, (GLA_WIDTH, GLA_WIDTH), 0) // GLA_DV
    gc = lax.broadcasted_iota(jnp.int32, (GLA_WIDTH, GLA_WIDTH), 1) // GLA_DV
    gmat = jnp.where(gr == gc, 1.0 / GLA_DV, 0.0).astype(BF16)

    def fin(t, _):
        rows = pl.ds(pl.multiple_of(t * rt, rt), rt)
        o = od_ref[rows, :]
        ms = jnp.dot((o * o).astype(BF16), gmat, preferred_element_type=F32)
        o = o * lax.rsqrt(ms + EPS) * ng_ref[...]
        o_ref[0, rows, :] = (o * _silu(gg_ref[0, rows, :])).astype(BF16)
        return 0

    lax.fori_loop(0, seq // rt, fin, 0)


def _gla_call(gqk, gv, gg, glr, wdec_pad, bdec, ng4):
    b, l, _ = gqk.shape
    rt = min(256, l)
    once = dict(pipeline_mode=pl.Buffered(1))
    bmap = lambda bi: (bi, 0, 0)
    return pl.pallas_call(
        functools.partial(_gla_kernel, rt=rt),
        grid=(b,),
        in_specs=[
            pl.BlockSpec((1, l, 2 * GLA_QK), bmap, **once),
            pl.BlockSpec((1, l, GLA_WIDTH), bmap, **once),
            pl.BlockSpec((1, l, GLA_WIDTH), bmap, **once),
            pl.BlockSpec((1, l, LANES), bmap, **once),
            pl.BlockSpec((2, LANES, GLA_QK), lambda bi: (0, 0, 0)),
            pl.BlockSpec((2, 1, GLA_QK), lambda bi: (0, 0, 0)),
            pl.BlockSpec((1, GLA_WIDTH), lambda bi: (0, 0)),
        ],
        out_specs=pl.BlockSpec((1, l, GLA_WIDTH), bmap),
        out_shape=SDS((b, l, GLA_WIDTH), BF16),
        scratch_shapes=[
            pltpu.VMEM((2, l, GLA_QK), F32),
            pltpu.VMEM((2, l, GLA_QK), F32),
            pltpu.VMEM((2, l, GLA_QK), F32),
            pltpu.VMEM((l, GLA_WIDTH), F32),
            pltpu.VMEM((2, GLA_WIDTH, GLA_QK), F32),
        ],
        compiler_params=_params("parallel"),
        name="gla",
    )(gqk, gv, gg, glr, wdec_pad, bdec, ng4)


def _hy_pre_kernel(x0_ref, x1_ref, v_ref, w0_ref, w1_ref, wv_ref, b0_ref, b1_ref, bv_ref, z_ref, x0o_ref, *, rt):
    seq = x0_ref.shape[1]
    rowid = lax.broadcasted_iota(jnp.int32, (rt, LANES), 0)

    def conv(ref, w_ref, b_ref, r0):
        cur = ref[0, pl.ds(r0, rt), :]
        prev = ref[0, pl.ds(pl.multiple_of(jnp.maximum(r0 - 8, 0), 8), 8), :][7:8, :]
        nxt = ref[0, pl.ds(pl.multiple_of(jnp.minimum(r0 + rt, seq - 8), 8), 8), :][0:1, :]
        prev = jnp.where(r0 > 0, prev, 0.0)
        nxt = jnp.where(r0 + rt < seq, nxt, 0.0)
        up = jnp.where(rowid == 0, prev, pltpu.roll(cur, 1, 0))
        down = jnp.where(rowid == rt - 1, nxt, pltpu.roll(cur, rt - 1, 0))
        return up * w_ref[0:1, :] + cur * w_ref[1:2, :] + down * w_ref[2:3, :] + b_ref[...]

    def body(t, _):
        r0 = pl.multiple_of(t * rt, rt)
        rows = pl.ds(r0, rt)
        x0o_ref[0, rows, :] = conv(x0_ref, w0_ref, b0_ref, r0)
        z_ref[0, rows, :] = conv(v_ref, wv_ref, bv_ref, r0) * conv(x1_ref, w1_ref, b1_ref, r0)
        return 0

    lax.fori_loop(0, seq // rt, body, 0)


def _hy_pre_call(hy, conv_w, conv_b):
    b, l, w3 = hy.shape
    hw = w3 // 3
    nj = hw // LANES
    rt = min(512, l)
    xs = lambda off: pl.BlockSpec((1, l, LANES), lambda bi, j: (bi, 0, off * nj + j))
    ws = lambda off: pl.BlockSpec((3, LANES), lambda bi, j: (0, off * nj + j))
    bs = lambda off: pl.BlockSpec((1, LANES), lambda bi, j: (0, off * nj + j))
    out = pl.BlockSpec((1, l, LANES), lambda bi, j: (bi, 0, j))
    return pl.pallas_call(
        functools.partial(_hy_pre_kernel, rt=rt),
        grid=(b, nj),
        in_specs=[xs(0), xs(1), xs(2), ws(0), ws(1), ws(2), bs(0), bs(1), bs(2)],
        out_specs=[out, out],
        out_shape=[SDS((b, l, hw), F32), SDS((b, l, hw), F32)],
        compiler_params=_params("parallel", "parallel"),
        name="hyena_short_conv",
    )(hy, hy, hy, conv_w, conv_w, conv_w, conv_b, conv_b, conv_b)


def _hy_filter_kernel(z_ref, w1_ref, b1_ref, w2_ref, b2_ref, w3_ref, fr_ref, dl_ref, o_ref, *, seq, tr):
    z = z_ref[...]
    fr = fr_ref[...]
    h = jnp.sin(fr * (jnp.dot(z, w1_ref[...], precision=HI, preferred_element_type=F32) + b1_ref[...]))
    for i in range(w2_ref.shape[0]):
        h = jnp.sin(fr * (jnp.dot(h, w2_ref[i], precision=HI, preferred_element_type=F32) + b2_ref[i]))
    hw = jnp.dot(h, w3_ref[...], precision=HI, preferred_element_type=F32)
    c = hw.shape[1] // 2
    window = jnp.exp(-z[:, 0:1] * dl_ref[...])
    m = pl.program_id(0) * tr + lax.broadcasted_iota(jnp.int32, (tr, c), 0)
    tap = jnp.where(m >= seq, hw[:, :c], hw[:, c:]) * window
    o_ref[...] = jnp.where(m == 0, 0.0, tap)


def _hy_filter_call(feat, w1p, b1, w2, b2, w3, freq, deltas, seq):
    n2, _ = feat.shape
    od = w1p.shape[1]
    c2 = w3.shape[1]
    tr = min(1024, n2)
    full = lambda *shape: pl.BlockSpec(shape, lambda i: (0,) * len(shape))
    return pl.pallas_call(
        functools.partial(_hy_filter_kernel, seq=seq, tr=tr),
        grid=(n2 // tr,),
        in_specs=[
            pl.BlockSpec((tr, LANES), lambda i: (i, 0)),
            full(LANES, od), full(1, od), full(w2.shape[0], od, od), full(w2.shape[0], 1, od),
            full(od, c2), full(1, od), full(1, c2 // 2),
        ],
        out_specs=pl.BlockSpec((tr, c2 // 2), lambda i: (i, 0)),
        out_shape=SDS((n2, c2 // 2), F32),
        compiler_params=_params("parallel"),
        name="hyena_filter",
    )(feat, w1p, b1, w2, b2, w3, freq, deltas)


def _hy_conv_kernel(z_ref, f_ref, y_ref, acc_ref, *, nb, bp):
    cg = z_ref.shape[0]
    p = HY_BLOCK
    ii = lax.broadcasted_iota(jnp.int32, (p, p), 0)
    jj = lax.broadcasted_iota(jnp.int32, (p, p), 1)
    upper = jj >= ii

    def chan(ci, _):
        z = z_ref[ci]
        acc_ref[...] = jnp.zeros(acc_ref.shape, F32)
        for k in range(2 * nb - 1):
            delta = k - (nb - 1)
            lo = jnp.broadcast_to(f_ref[ci, k:k + 1, :], (p, p))
            hi = jnp.broadcast_to(f_ref[ci, k + 1:k + 2, :], (p, p))
            w = jnp.where(upper, pltpu.roll(hi, 0, 1, stride=1, stride_axis=0),
                          pltpu.roll(lo, 0, 1, stride=1, stride_axis=0)).astype(BF16)
            n = (nb - abs(delta)) * bp
            src = max(0, -delta) * bp
            dst = max(0, delta) * bp
            acc_ref[dst:dst + n, :] += jnp.dot(z[src:src + n, :].astype(BF16), w, preferred_element_type=F32)
        y_ref[ci] = acc_ref[...]
        return 0

    lax.fori_loop(0, cg, chan, 0)


def _hy_conv_call(zt, ft, nb, bp):
    c, rows, p = zt.shape
    cg = 8
    return pl.pallas_call(
        functools.partial(_hy_conv_kernel, nb=nb, bp=bp),
        grid=(c // cg,),
        in_specs=[
            pl.BlockSpec((cg, rows, p), lambda g: (g, 0, 0)),
            pl.BlockSpec((cg, 2 * nb, p), lambda g: (g, 0, 0)),
        ],
        out_specs=pl.BlockSpec((cg, rows, p), lambda g: (g, 0, 0)),
        out_shape=SDS((c, rows, p), F32),
        scratch_shapes=[pltpu.VMEM((rows, p), F32)],
        compiler_params=_params("parallel"),
        name="hyena_long_conv",
    )(zt, ft)


def _outproj_kernel(x_ref, ogla_ref, oatt_ref, yc_ref, z_ref, x0_ref, skip_ref, w_ref, mod_ref,
                    gpost_ref, gpre_ref, rw_ref, xo_ref, h2_ref, aff_ref, *, n_exp):
    ohy = (x0_ref[0] * (yc_ref[0] + z_ref[0] * skip_ref[...])).astype(BF16)
    g0 = GLA_WIDTH
    g1 = GLA_WIDTH + ATT_WIDTH
    y = jnp.dot(ogla_ref[0], w_ref[0:g0, :], preferred_element_type=F32)
    y = y + jnp.dot(oatt_ref[0], w_ref[g0:g1, :], preferred_element_type=F32)
    y = y + jnp.dot(ohy, w_ref[g1:, :], preferred_element_type=F32)
    g_m = mod_ref[0, 2:3, :]
    sh_f = mod_ref[0, 3:4, :]
    sc_f = mod_ref[0, 4:5, :]
    x1 = x_ref[0] + g_m * (_rms(y) * gpost_ref[...])
    xo_ref[0] = x1
    h2 = _rms(x1) * gpre_ref[...] * (1.0 + sc_f) + sh_f
    h2_ref[0] = h2.astype(BF16)
    h_hi = h2.astype(BF16)
    h_lo = (h2 - h_hi.astype(F32)).astype(BF16)
    logits = (jnp.dot(h_hi, rw_ref[0], preferred_element_type=F32)
              + jnp.dot(h_lo, rw_ref[0], preferred_element_type=F32)
              + jnp.dot(h_hi, rw_ref[1], preferred_element_type=F32))
    lane = lax.broadcasted_iota(jnp.int32, logits.shape, 1)
    logits = jnp.where(lane < n_exp, logits, -jnp.inf)
    e = jnp.exp(logits - jnp.max(logits, axis=-1, keepdims=True))
    aff_ref[0] = e / jnp.sum(e, axis=-1, keepdims=True)


def _outproj_call(x, ogla, oatt, yc, z, x0, skip, w_out, mod_l, gpost, gpre, rw_pad, n_exp):
    b, l, d = x.shape
    tm = min(512, l)
    row = lambda bi, i: (bi, i, 0)
    full2 = lambda bi, i: (0, 0)
    hw = yc.shape[-1]
    return pl.pallas_call(
        functools.partial(_outproj_kernel, n_exp=n_exp),
        grid=(b, l // tm),
        in_specs=[
            pl.BlockSpec((1, tm, d), row),
            pl.BlockSpec((1, tm, GLA_WIDTH), row),
            pl.BlockSpec((1, tm, ATT_WIDTH), row),
            pl.BlockSpec((1, tm, hw), row),
            pl.BlockSpec((1, tm, hw), row),
            pl.BlockSpec((1, tm, hw), row),
            pl.BlockSpec((1, hw), full2),
            pl.BlockSpec((d, d), full2),
            pl.BlockSpec((1, 6, d), lambda bi, i: (bi, 0, 0)),
            pl.BlockSpec((1, d), full2),
            pl.BlockSpec((1, d), full2),
            pl.BlockSpec((2, d, LANES), lambda bi, i: (0, 0, 0)),
        ],
        out_specs=[pl.BlockSpec((1, tm, d), row), pl.BlockSpec((1, tm, d), row), pl.BlockSpec((1, tm, LANES), row)],
        out_shape=[SDS((b, l, d), F32), SDS((b, l, d), BF16), SDS((b, l, LANES), F32)],
        compiler_params=_params("parallel", "parallel"),
        name="out_proj_router",
    )(x, ogla, oatt, yc, z, x0, skip, w_out, mod_l, gpost, gpre, rw_pad)


def _route_kernel(aff_ref, pos_ref, *, cap):
    aff = aff_ref[0]
    n_exp, t = aff.shape

    def bisect(_, bounds):
        lo, hi = bounds
        mid = 0.5 * (lo + hi)
        ok = jnp.sum(jnp.where(aff >= mid, 1.0, 0.0), axis=1, keepdims=True) >= cap
        return jnp.where(ok, mid, lo), jnp.where(ok, hi, mid)

    lo, hi = lax.fori_loop(0, ROUTE_BISECTIONS, bisect,
                           (jnp.zeros((n_exp, 1), F32), jnp.full((n_exp, 1), 2.0, F32)))
    need = cap - jnp.sum(jnp.where(aff >= hi, 1.0, 0.0), axis=1, keepdims=True)
    r = lax.broadcasted_iota(jnp.int32, (LANES, LANES), 0)
    c = lax.broadcasted_iota(jnp.int32, (LANES, LANES), 1)
    tri = jnp.where(r <= c, 1.0, 0.0).astype(BF16)
    eq_run = jnp.zeros((n_exp, 1), F32)
    sel_run = jnp.zeros((n_exp, 1), F32)
    for j in range(t // LANES):
        sl = slice(j * LANES, (j + 1) * LANES)
        aj = aff[:, sl]
        eq_j = jnp.where(aj >= hi, 0.0, jnp.where(aj >= lo, 1.0, 0.0))
        eq_cum = jnp.dot(eq_j.astype(BF16), tri, preferred_element_type=F32) + eq_run
        sel_j = jnp.where(aj >= hi, 1.0, jnp.where(eq_cum <= need, eq_j, 0.0))
        sel_cum = jnp.dot(sel_j.astype(BF16), tri, preferred_element_type=F32) + sel_run
        pos_ref[0, :, sl] = jnp.where(sel_j > 0.0, sel_cum - 1.0, -1.0).astype(jnp.int32)
        eq_run = eq_cum[:, LANES - 1:LANES]
        sel_run = sel_cum[:, LANES - 1:LANES]


def _route_call(aff_t, cap):
    b, n_exp, t = aff_t.shape
    return pl.pallas_call(
        functools.partial(_route_kernel, cap=cap),
        grid=(b,),
        in_specs=[pl.BlockSpec((1, n_exp, t), lambda bi: (bi, 0, 0))],
        out_specs=pl.BlockSpec((1, n_exp, t), lambda bi: (bi, 0, 0)),
        out_shape=SDS((b, n_exp, t), jnp.int32),
        compiler_params=_params("parallel"),
        name="ec_route",
    )(aff_t)


def _dispatch_kernel(pos_ref, aff_ref, h_ref, xg_ref, gc_ref, *, cap, tc):
    t = h_ref.shape[1]
    d = h_ref.shape[2]
    slot = lax.broadcasted_iota(jnp.int32, (cap, tc), 0)

    def body(i, carry):
        acc, gacc = carry
        off = pl.multiple_of(i * tc, tc)
        hit = slot == pos_ref[0, :, pl.ds(off, tc)]
        sel = jnp.where(hit, 1.0, 0.0).astype(BF16)
        acc = acc + jnp.dot(sel, h_ref[0, pl.ds(off, tc), :], preferred_element_type=F32)
        gacc = gacc + jnp.sum(jnp.where(hit, aff_ref[0, :, pl.ds(off, tc)], 0.0), axis=1, keepdims=True)
        return acc, gacc

    acc, gacc = lax.fori_loop(0, t // tc, body, (jnp.zeros((cap, d), F32), jnp.zeros((cap, 1), F32)))
    xg_ref[0] = acc.astype(BF16)
    gc_ref[0] = jnp.broadcast_to(gacc, (cap, LANES))


def _dispatch_call(pos_rows, aff_rows, h2, n_exp, cap):
    b, t, d = h2.shape
    tc = min(512, t)
    return pl.pallas_call(
        functools.partial(_dispatch_kernel, cap=cap, tc=tc),
        grid=(b, n_exp),
        in_specs=[
            pl.BlockSpec((1, 1, t), lambda bi, e: (bi * n_exp + e, 0, 0)),
            pl.BlockSpec((1, 1, t), lambda bi, e: (bi * n_exp + e, 0, 0)),
            pl.BlockSpec((1, t, d), lambda bi, e: (bi, 0, 0)),
        ],
        out_specs=[pl.BlockSpec((1, cap, d), lambda bi, e: (e, bi, 0)),
                   pl.BlockSpec((1, cap, LANES), lambda bi, e: (e, bi, 0))],
        out_shape=[SDS((n_exp, b * cap, d), BF16), SDS((n_exp, b * cap, LANES), F32)],
        compiler_params=_params("parallel", "arbitrary"),
        name="ec_dispatch",
    )(pos_rows, aff_rows, h2)


def _ffn_kernel(xg_ref, gc_ref, wg_ref, wu_ref, wd_ref, y_ref, acc_ref, *, mt):
    f = pl.program_id(1)
    wg = wg_ref[0, 0].astype(BF16)
    wu = wu_ref[0, 0].astype(BF16)
    wd = wd_ref[0, 0].astype(BF16)
    m = xg_ref.shape[1]
    for i in range(m // mt):
        rows = slice(i * mt, (i + 1) * mt)
        xb = xg_ref[0, rows, :]
        a = jnp.dot(xb, wg, preferred_element_type=F32)
        u = jnp.dot(xb, wu, preferred_element_type=F32)
        contrib = jnp.dot((_silu(a) * u).astype(BF16), wd, preferred_element_type=F32)

        @pl.when(f == 0)
        def _():
            acc_ref[rows, :] = contrib

        @pl.when(f > 0)
        def _():
            acc_ref[rows, :] += contrib

    @pl.when(f == pl.num_programs(1) - 1)
    def _():
        y_ref[0] = (acc_ref[...] * gc_ref[0][:, 0:1]).astype(BF16)


def _ffn_call(xg, gc, w_gate, w_up, w_down, layer):
    n_exp, m, d = xg.shape
    ff = w_gate.shape[3]
    tf = 256
    mt = m
    return pl.pallas_call(
        functools.partial(_ffn_kernel, mt=mt),
        grid=(n_exp, ff // tf),
        in_specs=[
            pl.BlockSpec((1, m, d), lambda e, f: (e, 0, 0)),
            pl.BlockSpec((1, m, LANES), lambda e, f: (e, 0, 0)),
            pl.BlockSpec((1, 1, d, tf), lambda e, f: (layer, e, 0, f)),
            pl.BlockSpec((1, 1, d, tf), lambda e, f: (layer, e, 0, f)),
            pl.BlockSpec((1, 1, tf, d), lambda e, f: (layer, e, f, 0)),
        ],
        out_specs=pl.BlockSpec((1, m, d), lambda e, f: (e, 0, 0)),
        out_shape=SDS((n_exp, m, d), BF16),
        scratch_shapes=[pltpu.VMEM((m, d), F32)],
        compiler_params=_params("parallel", "arbitrary"),
        name="expert_ffn",
    )(xg, gc, w_gate, w_up, w_down)


def _combine_kernel(pos_ref, y_ref, x_ref, mod_ref, g_ref, o_ref, acc_ref, *, cap, tc):
    e = pl.program_id(2)
    tt = x_ref.shape[1]
    slot = lax.broadcasted_iota(jnp.int32, (cap, tc), 0)
    yb = y_ref[0]

    @pl.when(e == 0)
    def _():
        acc_ref[...] = jnp.zeros(acc_ref.shape, F32)

    for i in range(tt // tc):
        sl = slice(i * tc, (i + 1) * tc)
        sel = jnp.where(slot == pos_ref[0, :, sl], 1.0, 0.0).astype(BF16)
        acc_ref[sl, :] += lax.dot_general(sel, yb, (((0,), (0,)), ((), ())), preferred_element_type=F32)

    @pl.when(e == pl.num_programs(2) - 1)
    def _():
        g_f = mod_ref[0, 5:6, :]
        o_ref[0] = x_ref[0] + g_f * (_rms(acc_ref[...]) * g_ref[...])


def _combine_call(pos_rows, y, x, mod_l, gpost, n_exp, cap):
    b, t, d = x.shape
    tt = min(1024, t)
    tc = min(512, tt)
    return pl.pallas_call(
        functools.partial(_combine_kernel, cap=cap, tc=tc),
        grid=(b, t // tt, n_exp),
        in_specs=[
            pl.BlockSpec((1, 1, tt), lambda bi, i, e: (bi * n_exp + e, 0, i)),
            pl.BlockSpec((1, cap, d), lambda bi, i, e: (e, bi, 0)),
            pl.BlockSpec((1, tt, d), lambda bi, i, e: (bi, i, 0)),
            pl.BlockSpec((1, 6, d), lambda bi, i, e: (bi, 0, 0)),
            pl.BlockSpec((1, d), lambda bi, i, e: (0, 0)),
        ],
        out_specs=pl.BlockSpec((1, tt, d), lambda bi, i, e: (bi, i, 0)),
        out_shape=SDS((b, t, d), F32),
        scratch_shapes=[pltpu.VMEM((tt, d), F32)],
        compiler_params=_params("parallel", "parallel", "arbitrary"),
        name="ec_combine",
    )(pos_rows, y, x, mod_l, gpost)


def _rope_tables(seq):
    n_rows = seq // GRID_W
    row = jnp.repeat(jnp.arange(n_rows), GRID_W).astype(F32)
    col = jnp.tile(jnp.arange(GRID_W), n_rows).astype(F32)
    inv = 1.0 / (ROPE_THETA ** (jnp.arange(0, ROPE_HALF, 2, dtype=F32) / ROPE_HALF))
    ar = row[:, None] * inv[None]
    ac = col[:, None] * inv[None]
    ang = jnp.concatenate([ar, ar, ac, ac], axis=-1)
    cos, sin = jnp.cos(ang), jnp.sin(ang)
    first_half = (jnp.arange(HEAD_DIM) % ROPE_HALF) < (ROPE_HALF // 2)
    sin_signed = jnp.where(first_half[None], -sin, sin)
    return jnp.tile(cos, (1, 2)), jnp.tile(sin_signed, (1, 2))


def _filter_features(seq):
    pos = jnp.abs(jnp.arange(2 * seq) - seq).astype(F32)
    t = pos / (seq - 1)
    f = jnp.linspace(1e-4, HY_BANDS - 1, HY_BANDS, dtype=F32)
    ang = (2.0 * math.pi * pos / seq)[:, None] * f[None]
    z = jnp.concatenate([t[:, None], jnp.cos(ang), -jnp.sin(ang)], axis=-1)
    return jnp.pad(z, ((0, 0), (0, LANES - HY_EMB)))


def kernel(x, c, ada_w, ada_b, mix_pre_g, mix_post_g, w_in, gla_w_dec, gla_b_dec, gla_norm_g, q_norm_g, k_norm_g, hy_conv_w, hy_conv_b, hy_w1, hy_b1, hy_w2, hy_b2, hy_w3, hy_freq, hy_skip, w_out, ffn_pre_g, ffn_post_g, router_w, exp_w_gate, exp_w_up, exp_w_down):
    b, seq, d = x.shape
    depth = ada_w.shape[0]
    n_exp = router_w.shape[-1]
    cap = EC_FACTOR * seq // n_exp
    hw = hy_skip.shape[-1]
    nb = seq // HY_BLOCK
    bp = 8
    assert b <= bp and seq % HY_BLOCK == 0

    mod = _ada_call(c, ada_w, ada_b).reshape(depth, b, 6, d)
    cos2, sin2 = _rope_tables(seq)
    feat = _filter_features(seq)
    deltas = jnp.abs(jnp.linspace(HY_MIN_DECAY, HY_MAX_DECAY, hw, dtype=F32))[None]

    for l in range(depth):
        w = w_in[l]
        w_pack = jnp.concatenate(
            [w[:, :768], jnp.pad(w[:, 768:800], ((0, 0), (0, LANES - 2 * GLA_LOWRANK))), w[:, 800:]], axis=1
        ).astype(BF16)
        gqk, gv, gg, glr, aq, ak, av, hy = _inproj_call(
            x, mod[l], mix_pre_g[l][None], w_pack, cos2, sin2,
            jnp.tile(q_norm_g[l], 2)[None], jnp.tile(k_norm_g[l], 2)[None])
        o_att = _attn_call(aq, ak, av)

        wdec_pad = jnp.zeros((2, LANES, GLA_QK), F32)
        wdec_pad = wdec_pad.at[0, :GLA_LOWRANK].set(gla_w_dec[l, 0]).at[1, GLA_LOWRANK:2 * GLA_LOWRANK].set(gla_w_dec[l, 1])
        o_gla = _gla_call(gqk, gv, gg, glr, wdec_pad, gla_b_dec[l][:, None, :],
                          jnp.tile(gla_norm_g[l], GLA_HEADS)[None])

        z, x0 = _hy_pre_call(hy, hy_conv_w[l], hy_conv_b[l][None])
        w1p = jnp.pad(hy_w1[l], ((0, LANES - HY_EMB), (0, 0)))
        filt = _hy_filter_call(feat, w1p, hy_b1[l][None], hy_w2[l], hy_b2[l][:, None, :], hy_w3[l],
                               hy_freq[l][None], deltas, seq)
        ft = filt.T.reshape(hw, 2 * nb, HY_BLOCK)
        zt = jnp.transpose(z.reshape(b, nb, HY_BLOCK, hw), (3, 1, 0, 2))
        zt = jnp.pad(zt, ((0, 0), (0, 0), (0, bp - b), (0, 0))).reshape(hw, nb * bp, HY_BLOCK)
        yt = _hy_conv_call(zt, ft, nb, bp).reshape(hw, nb, bp, HY_BLOCK)[:, :, :b]
        yc = jnp.transpose(yt, (2, 1, 3, 0)).reshape(b, seq, hw)

        rw_full = jnp.pad(router_w[l], ((0, 0), (0, LANES - n_exp)))
        rw_hi = rw_full.astype(BF16)
        rw_pad = jnp.stack([rw_hi, (rw_full - rw_hi.astype(F32)).astype(BF16)])
        x, h2, aff = _outproj_call(x, o_gla, o_att, yc, z, x0, hy_skip[l][None], w_out[l].astype(BF16), mod[l],
                                   mix_post_g[l][None], ffn_pre_g[l][None], rw_pad, n_exp)

        aff_t = jnp.swapaxes(aff[..., :n_exp], 1, 2)
        pos = _route_call(aff_t, cap)
        pos_rows = pos.reshape(b * n_exp, 1, seq)
        aff_rows = aff_t.reshape(b * n_exp, 1, seq)
        xg, gc = _dispatch_call(pos_rows, aff_rows, h2, n_exp, cap)
        y = _ffn_call(xg, gc, exp_w_gate, exp_w_up, exp_w_down, l)
        x = _combine_call(pos_rows, y, x, mod[l], ffn_post_g[l][None], n_exp, cap)
    return x
```

```python
import functools
import math

import jax
import jax.numpy as jnp
import numpy as np
from jax import lax
from jax.experimental import pallas as pl
from jax.experimental.pallas import tpu as pltpu

F32 = jnp.float32
BF16 = jnp.bfloat16
HI = lax.Precision.HIGHEST
SDS = jax.ShapeDtypeStruct

EPS = 1e-6
GRID_W = 64
HEAD_DIM = 64
GLA_HEADS = 4
GLA_DK = 32
GLA_DV = 64
GLA_QK = GLA_HEADS * GLA_DK
GLA_WIDTH = GLA_HEADS * GLA_DV
GLA_LOWRANK = 16
GLA_GATE_NORM = 16.0
GLA_BLOCK = 16
GLA_STEP_UNROLL = 4
ATT_HEADS = 8
ATT_KV_HEADS = 2
ATT_GROUP = ATT_HEADS // ATT_KV_HEADS
ATT_WIDTH = ATT_HEADS * HEAD_DIM
ATT_KV = ATT_KV_HEADS * HEAD_DIM
ATT_TK = 256
ATT_ONES_ROWS = 16
ROPE_THETA = 10000.0
ROPE_HALF = HEAD_DIM // 2
HY_EMB = 33
HY_BANDS = (HY_EMB - 1) // 2
HY_TARGET = 1e-2
HY_FAST = 0.3
HY_SLOW = 1.5
HY_MIN_DECAY = math.log(HY_TARGET) / HY_SLOW
HY_MAX_DECAY = math.log(HY_TARGET) / HY_FAST
HY_BLOCK = 128
N_EXPERTS = 16
EC_FACTOR = 2
ROUTE_BISECTIONS = 160
EC_SLOT_BLOCK = 256
EC_TOKEN_CHUNK = 512
LANES = 128
VMEM_LIMIT = 56 * 1024 * 1024


def _params(*sem):
    return pltpu.CompilerParams(dimension_semantics=sem, vmem_limit_bytes=VMEM_LIMIT)


def _silu(x):
    return x * (1.0 / (1.0 + jnp.exp(-x)))


def _rms(x):
    return x * lax.rsqrt(jnp.mean(x * x, axis=-1, keepdims=True) + EPS)


def _ada_kernel(c_ref, w_ref, b_ref, o_ref):
    cond = _silu(c_ref[...])
    o_ref[0] = jnp.dot(cond, w_ref[0], precision=HI, preferred_element_type=F32) + b_ref[0]


def _ada_call(c, ada_w, ada_b):
    depth, d, n6 = ada_w.shape
    b = c.shape[0]
    tn = 1024
    return pl.pallas_call(
        _ada_kernel,
        grid=(depth, n6 // tn),
        in_specs=[
            pl.BlockSpec((b, d), lambda l, j: (0, 0)),
            pl.BlockSpec((1, d, tn), lambda l, j: (l, 0, j)),
            pl.BlockSpec((1, 1, tn), lambda l, j: (l, 0, j)),
        ],
        out_specs=pl.BlockSpec((1, b, tn), lambda l, j: (l, 0, j)),
        out_shape=SDS((depth, b, n6), F32),
        compiler_params=_params("arbitrary", "arbitrary"),
        name="ada_mod",
    )(c, ada_w, ada_b.reshape(depth, 1, n6))


_C_GQK = (0, 256)
_C_GV = (256, 512)
_C_GG = (512, 768)
_C_GLR = (768, 896)
_C_AQ = (896, 1408)
_C_AK = (1408, 1536)
_C_AV = (1536, 1664)
_C_HY = (1664, 2432)
_NPACK = 2432


def _head_mean_matrix():
    r = lax.broadcasted_iota(jnp.int32, (LANES, LANES), 0) // HEAD_DIM
    c = lax.broadcasted_iota(jnp.int32, (LANES, LANES), 1) // HEAD_DIM
    return jnp.where(r == c, 1.0 / HEAD_DIM, 0.0).astype(BF16)


def _norm_rope(a, gain, cos, sin_signed, gmat, first_half):
    ms = jnp.dot((a * a).astype(BF16), gmat, preferred_element_type=F32)
    xn = a * lax.rsqrt(ms + EPS) * gain
    rot = jnp.where(first_half, pltpu.roll(xn, LANES - ROPE_HALF // 2, 1), pltpu.roll(xn, ROPE_HALF // 2, 1))
    return xn * cos + rot * sin_signed


def _inproj_kernel(x_ref, mod_ref, g_ref, w_ref, cos_ref, sin_ref, qg_ref, kg_ref,
                   gqk_ref, gv_ref, gg_ref, glr_ref, q_ref, k_ref, vt_ref, hy_ref, *, tk):
    x = x_ref[0]
    sh = mod_ref[0, 0:1, :]
    sc = mod_ref[0, 1:2, :]
    h = (_rms(x) * g_ref[...] * (1.0 + sc) + sh).astype(BF16)

    def seg(c):
        return jnp.dot(h, w_ref[:, c[0]:c[1]], preferred_element_type=F32)

    gqk_ref[0] = seg(_C_GQK)
    gv_ref[0] = seg(_C_GV)
    gg_ref[0] = seg(_C_GG)
    glr_ref[0] = seg(_C_GLR)
    hy_ref[0] = seg(_C_HY)
    av = seg(_C_AV)
    for ci in range(av.shape[0] // tk):
        vt_ref[0, ci] = av[ci * tk:(ci + 1) * tk, :].T.astype(BF16)

    gmat = _head_mean_matrix()
    cos = cos_ref[...]
    sin_signed = sin_ref[...]
    lane = lax.broadcasted_iota(jnp.int32, cos.shape, 1)
    first_half = (lane % ROPE_HALF) < (ROPE_HALF // 2)
    aq = seg(_C_AQ)
    scale = HEAD_DIM ** -0.5
    for j in range(ATT_WIDTH // LANES):
        qj = _norm_rope(aq[:, j * LANES:(j + 1) * LANES], qg_ref[...], cos, sin_signed, gmat, first_half)
        q_ref[0, :, j * LANES:(j + 1) * LANES] = (qj * scale).astype(BF16)
    k_ref[0] = _norm_rope(seg(_C_AK), kg_ref[...], cos, sin_signed, gmat, first_half).astype(BF16)


def _inproj_call(x, mod_l, g, w_pack, cos2, sin2, qg2, kg2):
    b, l, d = x.shape
    tm = min(512, l)
    tk = min(ATT_TK, l)
    row = lambda bi, i: (bi, i, 0)
    full2 = lambda bi, i: (0, 0)
    rows_out = [
        (SDS((b, l, 256), F32), 256), (SDS((b, l, 256), F32), 256), (SDS((b, l, 256), F32), 256),
        (SDS((b, l, 128), F32), 128), (SDS((b, l, ATT_WIDTH), BF16), ATT_WIDTH),
        (SDS((b, l, ATT_KV), BF16), ATT_KV),
    ]
    out_specs = [pl.BlockSpec((1, tm, w), row) for _, w in rows_out]
    out_shape = [s for s, _ in rows_out]
    out_specs.append(pl.BlockSpec((1, tm // tk, ATT_KV, tk), lambda bi, i: (bi, i, 0, 0)))
    out_shape.append(SDS((b, l // tk, ATT_KV, tk), BF16))
    out_specs.append(pl.BlockSpec((1, tm, 768), row))
    out_shape.append(SDS((b, l, 768), F32))
    return pl.pallas_call(
        functools.partial(_inproj_kernel, tk=tk),
        grid=(b, l // tm),
        in_specs=[
            pl.BlockSpec((1, tm, d), row),
            pl.BlockSpec((1, 6, d), lambda bi, i: (bi, 0, 0)),
            pl.BlockSpec((1, d), full2),
            pl.BlockSpec((d, _NPACK), full2),
            pl.BlockSpec((tm, LANES), lambda bi, i: (i, 0)),
            pl.BlockSpec((tm, LANES), lambda bi, i: (i, 0)),
            pl.BlockSpec((1, LANES), full2),
            pl.BlockSpec((1, LANES), full2),
        ],
        out_specs=out_specs,
        out_shape=out_shape,
        compiler_params=_params("parallel", "parallel"),
        name="in_proj",
    )(x, mod_l, g, w_pack, cos2, sin2, qg2, kg2)


def _attn_kernel(q_ref, k_ref, vt_ref, o_ref, acc_ref, s_ref, p_ref, *, tq):
    nchunk = vt_ref.shape[1]
    tk = vt_ref.shape[3]
    lane = lax.broadcasted_iota(jnp.int32, (tq, LANES), 1)
    lower = lane < HEAD_DIM
    ones_rows = ATT_ONES_ROWS
    vrows = HEAD_DIM + ones_rows
    row = lax.broadcasted_iota(jnp.int32, (vrows, tk), 0)
    q = q_ref[0].astype(F32)
    streams = []
    qts = []
    for j in range(ATT_KV_HEADS):
        keep = lower if j == 0 else jnp.logical_not(lower)
        for hp in range(ATT_GROUP // 2):
            cb = j * (ATT_GROUP // 2) + hp
            qc = q[:, cb * LANES:(cb + 1) * LANES]
            qsw = pltpu.roll(qc, HEAD_DIM, 1)
            first, second = (qc, qsw) if j == 0 else (qsw, qc)
            cols = [jnp.where(keep, first, 0.0).T, jnp.where(keep, second, 0.0).T]
            qts.append(jnp.concatenate(cols, axis=1).astype(BF16))
            streams.append((j, cb))
    qt_all = jnp.concatenate(qts, axis=1)
    half = ATT_GROUP * tq
    acc_ref[...] = jnp.zeros(acc_ref.shape, F32)
    p_ref[1] = jnp.zeros(p_ref.shape[1:], BF16)
    ones = jnp.ones((vrows, tk), BF16)

    def scores(c):
        kc = k_ref[0, pl.ds(pl.multiple_of(c * tk, tk), tk), :]
        return jnp.dot(kc, qt_all, preferred_element_type=F32)

    def add_values(c, slot, alpha):
        vt = vt_ref[0, c]
        for j in range(ATT_KV_HEADS):
            r0 = j * (LANES - vrows)
            vm = jnp.where((row < HEAD_DIM) if j == 0 else (row >= ones_rows), vt[r0:r0 + vrows, :], ones)
            pj = p_ref[slot, :, j * half:(j + 1) * half]
            acc_ref[j] = acc_ref[j] * alpha[:, j * half:(j + 1) * half] + jnp.dot(vm, pj, preferred_element_type=F32)

    s_ref[0] = scores(0)

    def step(c, slot, carry):
        m_prev, alpha_prev = carry
        s_ref[1 - slot] = scores(jnp.minimum(c + 1, nchunk - 1))
        add_values(jnp.maximum(c - 1, 0), 1 - slot, alpha_prev)
        st = s_ref[slot]
        m_new = jnp.maximum(m_prev, jnp.max(st, axis=0, keepdims=True))
        p_ref[slot] = jnp.exp((st - m_new).astype(BF16))
        return m_new, jnp.exp(m_prev - m_new)

    def body(i, carry):
        return step(2 * i + 1, 1, step(2 * i, 0, carry))

    assert nchunk % 2 == 0
    init = (jnp.full((1, 2 * half), -jnp.inf, F32), jnp.ones((1, 2 * half), F32))
    _, alpha = lax.fori_loop(0, nchunk // 2, body, init)
    add_values(nchunk - 1, 1, alpha)

    for j, cb in streams:
        hp = cb % (ATT_GROUP // 2)
        acc = acc_ref[j][:, hp * 2 * tq:(hp + 1) * 2 * tq]
        d0 = j * ones_rows
        denom = acc[(1 - j) * HEAD_DIM:(1 - j) * HEAD_DIM + 1, :]
        data = acc[d0:d0 + HEAD_DIM, :] / denom
        blk = jnp.concatenate([data[:, :tq], data[:, tq:]], axis=0)
        o_ref[0, :, cb * LANES:(cb + 1) * LANES] = blk.T.astype(BF16)


def _attn_call(q, k, vt):
    b, l, _ = q.shape
    tq = min(256, l)
    nchunk, tk = vt.shape[1], vt.shape[3]
    nq = ATT_HEADS * tq
    return pl.pallas_call(
        functools.partial(_attn_kernel, tq=tq),
        grid=(b, l // tq),
        in_specs=[
            pl.BlockSpec((1, tq, ATT_WIDTH), lambda bi, i: (bi, i, 0)),
            pl.BlockSpec((1, l, ATT_KV), lambda bi, i: (bi, 0, 0)),
            pl.BlockSpec((1, nchunk, ATT_KV, tk), lambda bi, i: (bi, 0, 0, 0)),
        ],
        out_specs=pl.BlockSpec((1, tq, ATT_WIDTH), lambda bi, i: (bi, i, 0)),
        out_shape=SDS((b, l, ATT_WIDTH), BF16),
        scratch_shapes=[pltpu.VMEM((ATT_KV_HEADS, HEAD_DIM + ATT_ONES_ROWS, nq // ATT_KV_HEADS), F32),
                        pltpu.VMEM((2, tk, nq), F32), pltpu.VMEM((2, tk, nq), BF16)],
        compiler_params=_params("parallel", "parallel"),
        name="gqa_attention",
    )(q, k, vt)


def _log_sigmoid(x):
    return jnp.minimum(x, 0.0) - jnp.log(1.0 + jnp.exp(-jnp.abs(x)))


def _gla_kernel(qk_ref, v_ref, gg_ref, lr_ref, wdec_ref, bdec_ref, ng_ref, o_ref,
                qs_ref, ks_ref, a_ref, od_ref, *, rt):
    seq = qk_ref.shape[1]
    cb = GLA_BLOCK
    rowmod = lax.broadcasted_iota(jnp.int32, (rt, GLA_QK), 0) % cb
    hd = lax.broadcasted_iota(jnp.int32, (GLA_QK, GLA_WIDTH), 0) // GLA_DK
    hc = lax.broadcasted_iota(jnp.int32, (GLA_QK, GLA_WIDTH), 1) // GLA_DV
    expand = jnp.where(hd == hc, 1.0, 0.0).astype(BF16)
    scale = GLA_DK ** -0.5

    def prep(t, _):
        r0 = pl.multiple_of(t * rt, rt)
        rows = pl.ds(r0, rt)
        q = qk_ref[0, rows, 0:GLA_QK] * scale
        k = qk_ref[0, rows, GLA_QK:2 * GLA_QK]
        v = v_ref[0, rows, :]
        lr = lr_ref[0, rows, :]
        od = jnp.zeros((rt, GLA_WIDTH), F32)
        for direction in range(2):
            la = _log_sigmoid(jnp.dot(lr, wdec_ref[direction], precision=HI, preferred_element_type=F32)
                              + bdec_ref[direction]) * (1.0 / GLA_GATE_NORM)
            pre = la
            suf = la
            s = 1
            while s < cb:
                pre = pre + jnp.where(rowmod >= s, pltpu.roll(pre, s, 0), 0.0)
                suf = suf + jnp.where(rowmod < cb - s, pltpu.roll(suf, rt - s, 0), 0.0)
                s *= 2
            tot = pre + suf - la
            cum = pre if direction == 0 else suf
            rem = (suf if direction == 0 else pre) - la
            qs_ref[direction, rows, :] = q * jnp.exp(cum)
            ks_ref[direction, rows, :] = k * jnp.exp(rem)
            a_ref[direction, rows, :] = jnp.exp(tot)
            for delta in range(cb):
                if delta == 0:
                    a_mat = q * k
                    vj = v
                else:
                    if direction == 0:
                        sh = delta
                        valid = rowmod >= delta
                    else:
                        sh = rt - delta
                        valid = rowmod < cb - delta
                    kj = pltpu.roll(k, sh, 0)
                    cj = pltpu.roll(cum, sh, 0)
                    vj = pltpu.roll(v, sh, 0)
                    a_mat = jnp.where(valid, q * kj * jnp.exp(cum - cj), 0.0)
                se = jnp.dot(a_mat.astype(BF16), expand, preferred_element_type=F32)
                od = od + se * vj
        od_ref[rows, :] = od
        return 0

    lax.fori_loop(0, seq // rt, prep, 0)

    sr = lax.broadcasted_iota(jnp.int32, (GLA_WIDTH, GLA_QK), 0) // GLA_DV
    scol = lax.broadcasted_iota(jnp.int32, (GLA_WIDTH, GLA_QK), 1) // GLA_DK
    same_head = sr == scol
    nblk = seq // cb
    unroll = GLA_STEP_UNROLL
    span = unroll * cb

    def step(i, states):
        states = list(states)
        for direction in range(2):
            base = i * span if direction == 0 else seq - (i + 1) * span
            rows = pl.ds(pl.multiple_of(base, span), span)
            qb = qs_ref[direction, rows, :].astype(BF16)
            kb = ks_ref[direction, rows, :].astype(BF16)
            vb = v_ref[0, rows, :].astype(BF16)
            ab = a_ref[direction, rows, :]
            st = states[direction]
            outs = [None] * unroll
            for u in (range(unroll) if direction == 0 else range(unroll - 1, -1, -1)):
                sl = slice(u * cb, (u + 1) * cb)
                outs[u] = lax.dot_general(qb[sl], st.astype(BF16), (((1,), (1,)), ((), ())),
                                          preferred_element_type=F32)
                kv = lax.dot_general(vb[sl], kb[sl], (((0,), (0,)), ((), ())), preferred_element_type=F32)
                st = st * ab[u * cb:u * cb + 1, :] + jnp.where(same_head, kv, 0.0)
            od_ref[rows, :] += jnp.concatenate(outs, axis=0)
            states[direction] = st
        return tuple(states)

    zero_state = jnp.zeros((GLA_WIDTH, GLA_QK), F32)
    lax.fori_loop(0, nblk // unroll, step, (zero_state, zero_state))

    gr = lax.broadcasted_iota(jnp.int32, (GLA_WIDTH, GLA_WIDTH), 0) // GLA_DV
    gc = lax.broadcasted_iota(jnp.int32, (GLA_WIDTH, GLA_WIDTH), 1) // GLA_DV
    gmat = jnp.where(gr == gc, 1.0 / GLA_DV, 0.0).astype(BF16)

    def fin(t, _):
        rows = pl.ds(pl.multiple_of(t * rt, rt), rt)
        o = od_ref[rows, :]
        ms = jnp.dot((o * o).astype(BF16), gmat, preferred_element_type=F32)
        o = o * lax.rsqrt(ms + EPS) * ng_ref[...]
        o_ref[0, rows, :] = (o * _silu(gg_ref[0, rows, :])).astype(BF16)
        return 0

    lax.fori_loop(0, seq // rt, fin, 0)


def _gla_call(gqk, gv, gg, glr, wdec_pad, bdec, ng4):
    b, l, _ = gqk.shape
    rt = min(256, l)
    once = dict(pipeline_mode=pl.Buffered(1))
    bmap = lambda bi: (bi, 0, 0)
    return pl.pallas_call(
        functools.partial(_gla_kernel, rt=rt),
        grid=(b,),
        in_specs=[
            pl.BlockSpec((1, l, 2 * GLA_QK), bmap, **once),
            pl.BlockSpec((1, l, GLA_WIDTH), bmap, **once),
            pl.BlockSpec((1, l, GLA_WIDTH), bmap, **once),
            pl.BlockSpec((1, l, LANES), bmap, **once),
            pl.BlockSpec((2, LANES, GLA_QK), lambda bi: (0, 0, 0)),
            pl.BlockSpec((2, 1, GLA_QK), lambda bi: (0, 0, 0)),
            pl.BlockSpec((1, GLA_WIDTH), lambda bi: (0, 0)),
        ],
        out_specs=pl.BlockSpec((1, l, GLA_WIDTH), bmap),
        out_shape=SDS((b, l, GLA_WIDTH), BF16),
        scratch_shapes=[
            pltpu.VMEM((2, l, GLA_QK), F32),
            pltpu.VMEM((2, l, GLA_QK), F32),
            pltpu.VMEM((2, l, GLA_QK), F32),
            pltpu.VMEM((l, GLA_WIDTH), F32),
        ],
        compiler_params=_params("parallel"),
        name="gla",
    )(gqk, gv, gg, glr, wdec_pad, bdec, ng4)


def _hy_pre_kernel(x0_ref, x1_ref, v_ref, w0_ref, w1_ref, wv_ref, b0_ref, b1_ref, bv_ref, z_ref, x0o_ref, *, rt):
    seq = x0_ref.shape[1]
    rowid = lax.broadcasted_iota(jnp.int32, (rt, LANES), 0)

    def conv(ref, w_ref, b_ref, r0):
        cur = ref[0, pl.ds(r0, rt), :]
        prev = ref[0, pl.ds(pl.multiple_of(jnp.maximum(r0 - 8, 0), 8), 8), :][7:8, :]
        nxt = ref[0, pl.ds(pl.multiple_of(jnp.minimum(r0 + rt, seq - 8), 8), 8), :][0:1, :]
        prev = jnp.where(r0 > 0, prev, 0.0)
        nxt = jnp.where(r0 + rt < seq, nxt, 0.0)
        up = jnp.where(rowid == 0, prev, pltpu.roll(cur, 1, 0))
        down = jnp.where(rowid == rt - 1, nxt, pltpu.roll(cur, rt - 1, 0))
        return up * w_ref[0:1, :] + cur * w_ref[1:2, :] + down * w_ref[2:3, :] + b_ref[...]

    def body(t, _):
        r0 = pl.multiple_of(t * rt, rt)
        rows = pl.ds(r0, rt)
        x0o_ref[0, rows, :] = conv(x0_ref, w0_ref, b0_ref, r0)
        z_ref[0, rows, :] = conv(v_ref, wv_ref, bv_ref, r0) * conv(x1_ref, w1_ref, b1_ref, r0)
        return 0

    lax.fori_loop(0, seq // rt, body, 0)


def _hy_pre_call(hy, conv_w, conv_b):
    b, l, w3 = hy.shape
    hw = w3 // 3
    nj = hw // LANES
    rt = min(512, l)
    xs = lambda off: pl.BlockSpec((1, l, LANES), lambda bi, j: (bi, 0, off * nj + j))
    ws = lambda off: pl.BlockSpec((3, LANES), lambda bi, j: (0, off * nj + j))
    bs = lambda off: pl.BlockSpec((1, LANES), lambda bi, j: (0, off * nj + j))
    out = pl.BlockSpec((1, l, LANES), lambda bi, j: (bi, 0, j))
    return pl.pallas_call(
        functools.partial(_hy_pre_kernel, rt=rt),
        grid=(b, nj),
        in_specs=[xs(0), xs(1), xs(2), ws(0), ws(1), ws(2), bs(0), bs(1), bs(2)],
        out_specs=[out, out],
        out_shape=[SDS((b, l, hw), F32), SDS((b, l, hw), F32)],
        compiler_params=_params("parallel", "parallel"),
        name="hyena_short_conv",
    )(hy, hy, hy, conv_w, conv_w, conv_w, conv_b, conv_b, conv_b)


def _hy_filter_kernel(z_ref, w1_ref, b1_ref, w2_ref, b2_ref, w3_ref, fr_ref, dl_ref, o_ref, *, seq, tr):
    z = z_ref[...]
    fr = fr_ref[...]
    h = jnp.sin(fr * (jnp.dot(z, w1_ref[...], precision=HI, preferred_element_type=F32) + b1_ref[...]))
    for i in range(w2_ref.shape[0]):
        h = jnp.sin(fr * (jnp.dot(h, w2_ref[i], precision=HI, preferred_element_type=F32) + b2_ref[i]))
    hw = jnp.dot(h, w3_ref[...], precision=HI, preferred_element_type=F32)
    c = hw.shape[1] // 2
    window = jnp.exp(-z[:, 0:1] * dl_ref[...])
    m = pl.program_id(0) * tr + lax.broadcasted_iota(jnp.int32, (tr, c), 0)
    tap = jnp.where(m >= seq, hw[:, :c], hw[:, c:]) * window
    o_ref[...] = jnp.where(m == 0, 0.0, tap)


def _hy_filter_call(feat, w1p, b1, w2, b2, w3, freq, deltas, seq):
    n2, _ = feat.shape
    od = w1p.shape[1]
    c2 = w3.shape[1]
    tr = min(1024, n2)
    full = lambda *shape: pl.BlockSpec(shape, lambda i: (0,) * len(shape))
    return pl.pallas_call(
        functools.partial(_hy_filter_kernel, seq=seq, tr=tr),
        grid=(n2 // tr,),
        in_specs=[
            pl.BlockSpec((tr, LANES), lambda i: (i, 0)),
            full(LANES, od), full(1, od), full(w2.shape[0], od, od), full(w2.shape[0], 1, od),
            full(od, c2), full(1, od), full(1, c2 // 2),
        ],
        out_specs=pl.BlockSpec((tr, c2 // 2), lambda i: (i, 0)),
        out_shape=SDS((n2, c2 // 2), F32),
        compiler_params=_params("parallel"),
        name="hyena_filter",
    )(feat, w1p, b1, w2, b2, w3, freq, deltas)


def _hy_conv_kernel(z_ref, f_ref, y_ref, acc_ref, *, nb, bp):
    cg = z_ref.shape[0]
    p = HY_BLOCK
    ii = lax.broadcasted_iota(jnp.int32, (p, p), 0)
    jj = lax.broadcasted_iota(jnp.int32, (p, p), 1)
    upper = jj >= ii

    def chan(ci, _):
        z = z_ref[ci]
        acc_ref[...] = jnp.zeros(acc_ref.shape, F32)
        def skew(k):
            return pltpu.roll(jnp.broadcast_to(f_ref[ci, k:k + 1, :], (p, p)), 0, 1, stride=1, stride_axis=0)

        hi = skew(0)
        for k in range(2 * nb - 1):
            delta = k - (nb - 1)
            lo, hi = hi, skew(k + 1)
            w = jnp.where(upper, hi, lo).astype(BF16)
            n = (nb - abs(delta)) * bp
            src = max(0, -delta) * bp
            dst = max(0, delta) * bp
            acc_ref[dst:dst + n, :] += jnp.dot(z[src:src + n, :].astype(BF16), w, preferred_element_type=F32)
        y_ref[ci] = acc_ref[...]
        return 0

    lax.fori_loop(0, cg, chan, 0)


def _hy_conv_call(zt, ft, nb, bp):
    c, rows, p = zt.shape
    cg = 8
    return pl.pallas_call(
        functools.partial(_hy_conv_kernel, nb=nb, bp=bp),
        grid=(c // cg,),
        in_specs=[
            pl.BlockSpec((cg, rows, p), lambda g: (g, 0, 0)),
            pl.BlockSpec((cg, 2 * nb, p), lambda g: (g, 0, 0)),
        ],
        out_specs=pl.BlockSpec((cg, rows, p), lambda g: (g, 0, 0)),
        out_shape=SDS((c, rows, p), F32),
        scratch_shapes=[pltpu.VMEM((rows, p), F32)],
        compiler_params=_params("parallel"),
        name="hyena_long_conv",
    )(zt, ft)


def _outproj_kernel(x_ref, ogla_ref, oatt_ref, yc_ref, z_ref, x0_ref, skip_ref, w_ref, mod_ref,
                    gpost_ref, gpre_ref, rw_ref, xo_ref, h2_ref, aff_ref, *, n_exp):
    ohy = (x0_ref[0] * (yc_ref[0] + z_ref[0] * skip_ref[...])).astype(BF16)
    g0 = GLA_WIDTH
    g1 = GLA_WIDTH + ATT_WIDTH
    y = jnp.dot(ogla_ref[0], w_ref[0:g0, :], preferred_element_type=F32)
    y = y + jnp.dot(oatt_ref[0], w_ref[g0:g1, :], preferred_element_type=F32)
    y = y + jnp.dot(ohy, w_ref[g1:, :], preferred_element_type=F32)
    g_m = mod_ref[0, 2:3, :]
    sh_f = mod_ref[0, 3:4, :]
    sc_f = mod_ref[0, 4:5, :]
    x1 = x_ref[0] + g_m * (_rms(y) * gpost_ref[...])
    xo_ref[0] = x1
    h2 = _rms(x1) * gpre_ref[...] * (1.0 + sc_f) + sh_f
    h2_ref[0] = h2.T.astype(BF16)
    h_hi = h2.astype(BF16)
    h_lo = (h2 - h_hi.astype(F32)).astype(BF16)
    logits = (jnp.dot(h_hi, rw_ref[0], preferred_element_type=F32)
              + jnp.dot(h_lo, rw_ref[0], preferred_element_type=F32)
              + jnp.dot(h_hi, rw_ref[1], preferred_element_type=F32))
    lane = lax.broadcasted_iota(jnp.int32, logits.shape, 1)
    logits = jnp.where(lane < n_exp, logits, -jnp.inf)
    e = jnp.exp(logits - jnp.max(logits, axis=-1, keepdims=True))
    aff_ref[0] = e / jnp.sum(e, axis=-1, keepdims=True)


def _outproj_call(x, ogla, oatt, yc, z, x0, skip, w_out, mod_l, gpost, gpre, rw_pad, n_exp):
    b, l, d = x.shape
    tm = min(512, l)
    row = lambda bi, i: (bi, i, 0)
    full2 = lambda bi, i: (0, 0)
    hw = yc.shape[-1]
    return pl.pallas_call(
        functools.partial(_outproj_kernel, n_exp=n_exp),
        grid=(b, l // tm),
        in_specs=[
            pl.BlockSpec((1, tm, d), row),
            pl.BlockSpec((1, tm, GLA_WIDTH), row),
            pl.BlockSpec((1, tm, ATT_WIDTH), row),
            pl.BlockSpec((1, tm, hw), row),
            pl.BlockSpec((1, tm, hw), row),
            pl.BlockSpec((1, tm, hw), row),
            pl.BlockSpec((1, hw), full2),
            pl.BlockSpec((d, d), full2),
            pl.BlockSpec((1, 6, d), lambda bi, i: (bi, 0, 0)),
            pl.BlockSpec((1, d), full2),
            pl.BlockSpec((1, d), full2),
            pl.BlockSpec((2, d, LANES), lambda bi, i: (0, 0, 0)),
        ],
        out_specs=[pl.BlockSpec((1, tm, d), row), pl.BlockSpec((1, d, tm), lambda bi, i: (bi, 0, i)),
                   pl.BlockSpec((1, tm, LANES), row)],
        out_shape=[SDS((b, l, d), F32), SDS((b, d, l), BF16), SDS((b, l, LANES), F32)],
        compiler_params=_params("parallel", "parallel"),
        name="out_proj_router",
    )(x, ogla, oatt, yc, z, x0, skip, w_out, mod_l, gpost, gpre, rw_pad)


def _route_kernel(aff_ref, pos_ref, cum_ref, *, cap):
    aff = aff_ref[0]
    n_exp, t = aff.shape

    def bisect(_, bounds):
        lo, hi = bounds
        mid = 0.5 * (lo + hi)
        ok = jnp.sum(jnp.where(aff >= mid, 1.0, 0.0), axis=1, keepdims=True) >= cap
        return jnp.where(ok, mid, lo), jnp.where(ok, hi, mid)

    lo, hi = lax.fori_loop(0, ROUTE_BISECTIONS, bisect,
                           (jnp.zeros((n_exp, 1), F32), jnp.full((n_exp, 1), 2.0, F32)))
    need = cap - jnp.sum(jnp.where(aff >= hi, 1.0, 0.0), axis=1, keepdims=True)
    r = lax.broadcasted_iota(jnp.int32, (LANES, LANES), 0)
    c = lax.broadcasted_iota(jnp.int32, (LANES, LANES), 1)
    tri = jnp.where(r <= c, 1.0, 0.0).astype(BF16)
    eq_run = jnp.zeros((n_exp, 1), F32)
    sel_run = jnp.zeros((n_exp, 1), F32)
    runs = []
    for j in range(t // LANES):
        sl = slice(j * LANES, (j + 1) * LANES)
        aj = aff[:, sl]
        eq_j = jnp.where(aj >= hi, 0.0, jnp.where(aj >= lo, 1.0, 0.0))
        eq_cum = jnp.dot(eq_j.astype(BF16), tri, preferred_element_type=F32) + eq_run
        sel_j = jnp.where(aj >= hi, 1.0, jnp.where(eq_cum <= need, eq_j, 0.0))
        sel_cum = jnp.dot(sel_j.astype(BF16), tri, preferred_element_type=F32) + sel_run
        pos_ref[0, :, sl] = jnp.where(sel_j > 0.0, sel_cum - 1.0, -1.0).astype(jnp.int32)
        eq_run = eq_cum[:, LANES - 1:LANES]
        sel_run = sel_cum[:, LANES - 1:LANES]
        runs.append(sel_run)
    cum_ref[0] = jnp.concatenate(runs, axis=1).astype(jnp.int32)


def _route_call(aff_t, cap):
    b, n_exp, t = aff_t.shape
    rows = b * n_exp
    pos, cum = pl.pallas_call(
        functools.partial(_route_kernel, cap=cap),
        grid=(1,),
        in_specs=[pl.BlockSpec((1, rows, t), lambda i: (0, 0, 0))],
        out_specs=[pl.BlockSpec((1, rows, t), lambda i: (0, 0, 0)),
                   pl.BlockSpec((1, rows, t // LANES), lambda i: (0, 0, 0))],
        out_shape=[SDS((1, rows, t), jnp.int32), SDS((1, rows, t // LANES), jnp.int32)],
        compiler_params=_params("arbitrary"),
        name="ec_route",
    )(aff_t.reshape(1, rows, t))
    return pos.reshape(rows, 1, t), cum.reshape(rows, t // LANES)


def _chunk_ranges(cum, cap, tc):
    per = tc // LANES
    cum_c = cum[:, per - 1::per]
    first = jnp.arange(0, cap, EC_SLOT_BLOCK, dtype=jnp.int32)
    chunk_of = lambda s: jnp.sum(cum_c[:, None, :] <= s[None, :, None], axis=-1).astype(jnp.int32)
    return chunk_of(first).reshape(-1), chunk_of(first + EC_SLOT_BLOCK - 1).reshape(-1)


def _one_hot(pos_ref, off, tc, first_slot):
    slot = first_slot + lax.broadcasted_iota(jnp.int32, (EC_SLOT_BLOCK, tc), 0)
    return jnp.where(slot == pos_ref[0, :, pl.ds(off, tc)], 1.0, 0.0).astype(BF16)


def _dispatch_kernel(lo_ref, hi_ref, pos_ref, aff_ref, ht_ref, xgt_ref, gc_ref, acc_ref, gacc_ref, *, tc):
    nr = pl.num_programs(2)
    r = pl.program_id(2)
    idx = (pl.program_id(0) * pl.num_programs(1) + pl.program_id(1)) * nr + r
    acc_ref[...] = jnp.zeros(acc_ref.shape, F32)
    gacc_ref[...] = jnp.zeros(gacc_ref.shape, F32)
    nt = (((1,), (1,)), ((), ()))

    def body(c, _):
        off = pl.multiple_of(c * tc, tc)
        sel = _one_hot(pos_ref, off, tc, r * EC_SLOT_BLOCK)
        acc_ref[...] += lax.dot_general(ht_ref[0, :, pl.ds(off, tc)], sel, nt, preferred_element_type=F32)
        g = aff_ref[0, :, pl.ds(off, tc)]
        g1 = g.astype(BF16).astype(F32)
        g2 = (g - g1).astype(BF16).astype(F32)
        g3 = g - g1 - g2
        prow = lax.broadcasted_iota(jnp.int32, (gacc_ref.shape[0], tc), 0)
        pieces = jnp.where(prow == 0, g1, jnp.where(prow == 1, g2, jnp.where(prow == 2, g3, 0.0))).astype(BF16)
        gacc_ref[...] += lax.dot_general(pieces, sel, nt, preferred_element_type=F32)
        return 0

    lax.fori_loop(lo_ref[idx], hi_ref[idx] + 1, body, 0)
    xgt_ref[0] = acc_ref[...].astype(BF16)
    gc_ref[0] = jnp.sum(gacc_ref[...], axis=0, keepdims=True)


def _dispatch_call(lo, hi, pos_rows, aff_rows, h2t, n_exp, cap):
    b, d, t = h2t.shape
    tc = min(EC_TOKEN_CHUNK, t)
    nr = cap // EC_SLOT_BLOCK
    row = lambda bi, e, r, lo_r, hi_r: (bi * n_exp + e, 0, 0)
    out = lambda bi, e, r, lo_r, hi_r: (e, 0, bi * nr + r)
    return pl.pallas_call(
        functools.partial(_dispatch_kernel, tc=tc),
        grid_spec=pltpu.PrefetchScalarGridSpec(
            num_scalar_prefetch=2,
            grid=(b, n_exp, nr),
            in_specs=[
                pl.BlockSpec((1, 1, t), row),
                pl.BlockSpec((1, 1, t), row),
                pl.BlockSpec((1, d, t), lambda bi, e, r, lo_r, hi_r: (bi, 0, 0)),
            ],
            out_specs=[pl.BlockSpec((1, d, EC_SLOT_BLOCK), out), pl.BlockSpec((1, 1, EC_SLOT_BLOCK), out)],
            scratch_shapes=[pltpu.VMEM((d, EC_SLOT_BLOCK), F32), pltpu.VMEM((16, EC_SLOT_BLOCK), F32)],
        ),
        out_shape=[SDS((n_exp, d, b * cap), BF16), SDS((n_exp, 1, b * cap), F32)],
        compiler_params=_params("parallel", "arbitrary", "arbitrary"),
        name="ec_dispatch",
    )(lo, hi, pos_rows, aff_rows, h2t)


def _ffn_kernel(xgt_ref, gc_ref, wg_ref, wu_ref, wd_ref, yt_ref, xg_ref, acc_ref, *, mt):
    f = pl.program_id(1)
    m = xg_ref.shape[0]

    @pl.when(f == 0)
    def _():
        acc_ref[...] = jnp.zeros(acc_ref.shape, F32)
        for i in range(m // mt):
            xg_ref[i * mt:(i + 1) * mt, :] = xgt_ref[0, :, i * mt:(i + 1) * mt].astype(F32).T.astype(BF16)

    wg = wg_ref[0, 0].astype(BF16)
    wu = wu_ref[0, 0].astype(BF16)
    wd = wd_ref[0, 0].astype(BF16)
    for i in range(m // mt):
        rows = slice(i * mt, (i + 1) * mt)
        xb = xg_ref[rows, :]
        a = jnp.dot(xb, wg, preferred_element_type=F32)
        u = jnp.dot(xb, wu, preferred_element_type=F32)
        acc_ref[rows, :] += jnp.dot((_silu(a) * u).astype(BF16), wd, preferred_element_type=F32)

    @pl.when(f == pl.num_programs(1) - 1)
    def _():
        for i in range(m // mt):
            cols = slice(i * mt, (i + 1) * mt)
            yt_ref[0, :, cols] = (acc_ref[cols, :].T * gc_ref[0, :, cols]).astype(BF16)


def _ffn_call(xgt, gc, w_gate, w_up, w_down, layer):
    n_exp, d, m = xgt.shape
    ff = w_gate.shape[3]
    tf = 256
    mt = min(512, m)
    return pl.pallas_call(
        functools.partial(_ffn_kernel, mt=mt),
        grid=(n_exp, ff // tf),
        in_specs=[
            pl.BlockSpec((1, d, m), lambda e, f: (e, 0, 0)),
            pl.BlockSpec((1, 1, m), lambda e, f: (e, 0, 0)),
            pl.BlockSpec((1, 1, d, tf), lambda e, f: (layer, e, 0, f)),
            pl.BlockSpec((1, 1, d, tf), lambda e, f: (layer, e, 0, f)),
            pl.BlockSpec((1, 1, tf, d), lambda e, f: (layer, e, f, 0)),
        ],
        out_specs=pl.BlockSpec((1, d, m), lambda e, f: (e, 0, 0)),
        out_shape=SDS((n_exp, d, m), BF16),
        scratch_shapes=[pltpu.VMEM((m, d), BF16), pltpu.VMEM((m, d), F32)],
        compiler_params=_params("parallel", "arbitrary"),
        name="expert_ffn",
    )(xgt, gc, w_gate, w_up, w_down)


def _combine_kernel(lo_ref, hi_ref, pos_ref, yt_ref, o_ref, *, tc):
    ne = pl.num_programs(1)
    nr = pl.num_programs(2)
    e = pl.program_id(1)
    r = pl.program_id(2)
    idx = (pl.program_id(0) * ne + e) * nr + r

    @pl.when(jnp.logical_and(e == 0, r == 0))
    def _():
        o_ref[...] = jnp.zeros(o_ref.shape, F32)

    yb = yt_ref[0]

    def body(c, _):
        off = pl.multiple_of(c * tc, tc)
        sel = _one_hot(pos_ref, off, tc, r * EC_SLOT_BLOCK)
        o_ref[0, :, pl.ds(off, tc)] += jnp.dot(yb, sel, preferred_element_type=F32)
        return 0

    lax.fori_loop(lo_ref[idx], hi_ref[idx] + 1, body, 0)


def _combine_call(lo, hi, pos_rows, yt, b, t, cap):
    n_exp, d, _ = yt.shape
    tc = min(EC_TOKEN_CHUNK, t)
    nr = cap // EC_SLOT_BLOCK
    return pl.pallas_call(
        functools.partial(_combine_kernel, tc=tc),
        grid_spec=pltpu.PrefetchScalarGridSpec(
            num_scalar_prefetch=2,
            grid=(b, n_exp, nr),
            in_specs=[
                pl.BlockSpec((1, 1, t), lambda bi, e, r, lo_r, hi_r: (bi * n_exp + e, 0, 0)),
                pl.BlockSpec((1, d, EC_SLOT_BLOCK), lambda bi, e, r, lo_r, hi_r: (e, 0, bi * nr + r)),
            ],
            out_specs=pl.BlockSpec((1, d, t), lambda bi, e, r, lo_r, hi_r: (bi, 0, 0)),
        ),
        out_shape=SDS((b, d, t), F32),
        compiler_params=_params("parallel", "arbitrary", "arbitrary"),
        name="ec_combine",
    )(lo, hi, pos_rows, yt)


def _ffn_residual_kernel(acc_ref, x_ref, mod_ref, g_ref, o_ref):
    y = acc_ref[0].T
    g_f = mod_ref[0, 5:6, :]
    o_ref[0] = x_ref[0] + g_f * (_rms(y) * g_ref[...])


def _ffn_residual_call(acc_t, x, mod_l, gpost):
    b, t, d = x.shape
    tt = min(512, t)
    return pl.pallas_call(
        _ffn_residual_kernel,
        grid=(b, t // tt),
        in_specs=[
            pl.BlockSpec((1, d, tt), lambda bi, i: (bi, 0, i)),
            pl.BlockSpec((1, tt, d), lambda bi, i: (bi, i, 0)),
            pl.BlockSpec((1, 6, d), lambda bi, i: (bi, 0, 0)),
            pl.BlockSpec((1, d), lambda bi, i: (0, 0)),
        ],
        out_specs=pl.BlockSpec((1, tt, d), lambda bi, i: (bi, i, 0)),
        out_shape=SDS((b, t, d), F32),
        compiler_params=_params("parallel", "parallel"),
        name="ffn_residual",
    )(acc_t, x, mod_l, gpost)


def _rope_tables(seq):
    n_rows = seq // GRID_W
    row = jnp.repeat(jnp.arange(n_rows), GRID_W).astype(F32)
    col = jnp.tile(jnp.arange(GRID_W), n_rows).astype(F32)
    inv = 1.0 / (ROPE_THETA ** (jnp.arange(0, ROPE_HALF, 2, dtype=F32) / ROPE_HALF))
    ar = row[:, None] * inv[None]
    ac = col[:, None] * inv[None]
    ang = jnp.concatenate([ar, ar, ac, ac], axis=-1)
    cos, sin = jnp.cos(ang), jnp.sin(ang)
    first_half = (jnp.arange(HEAD_DIM) % ROPE_HALF) < (ROPE_HALF // 2)
    sin_signed = jnp.where(first_half[None], -sin, sin)
    return jnp.tile(cos, (1, 2)), jnp.tile(sin_signed, (1, 2))


def _filter_features(seq):
    pos = jnp.abs(jnp.arange(2 * seq) - seq).astype(F32)
    t = pos / (seq - 1)
    f = jnp.linspace(1e-4, HY_BANDS - 1, HY_BANDS, dtype=F32)
    ang = (2.0 * math.pi * pos / seq)[:, None] * f[None]
    z = jnp.concatenate([t[:, None], jnp.cos(ang), -jnp.sin(ang)], axis=-1)
    return jnp.pad(z, ((0, 0), (0, LANES - HY_EMB)))


def kernel(x, c, ada_w, ada_b, mix_pre_g, mix_post_g, w_in, gla_w_dec, gla_b_dec, gla_norm_g, q_norm_g, k_norm_g, hy_conv_w, hy_conv_b, hy_w1, hy_b1, hy_w2, hy_b2, hy_w3, hy_freq, hy_skip, w_out, ffn_pre_g, ffn_post_g, router_w, exp_w_gate, exp_w_up, exp_w_down):
    b, seq, d = x.shape
    depth = ada_w.shape[0]
    n_exp = router_w.shape[-1]
    cap = EC_FACTOR * seq // n_exp
    hw = hy_skip.shape[-1]
    nb = seq // HY_BLOCK
    bp = 8
    assert b <= bp and seq % HY_BLOCK == 0

    mod = _ada_call(c, ada_w, ada_b).reshape(depth, b, 6, d)
    cos2, sin2 = _rope_tables(seq)
    feat = _filter_features(seq)
    deltas = jnp.abs(jnp.linspace(HY_MIN_DECAY, HY_MAX_DECAY, hw, dtype=F32))[None]

    for l in range(depth):
        w = w_in[l]
        w_pack = jnp.concatenate(
            [w[:, :768], jnp.pad(w[:, 768:800], ((0, 0), (0, LANES - 2 * GLA_LOWRANK))), w[:, 800:]], axis=1
        ).astype(BF16)
        gqk, gv, gg, glr, aq, ak, av, hy = _inproj_call(
            x, mod[l], mix_pre_g[l][None], w_pack, cos2, sin2,
            jnp.tile(q_norm_g[l], 2)[None], jnp.tile(k_norm_g[l], 2)[None])
        o_att = _attn_call(aq, ak, av)

        wdec_pad = jnp.zeros((2, LANES, GLA_QK), F32)
        wdec_pad = wdec_pad.at[0, :GLA_LOWRANK].set(gla_w_dec[l, 0]).at[1, GLA_LOWRANK:2 * GLA_LOWRANK].set(gla_w_dec[l, 1])
        o_gla = _gla_call(gqk, gv, gg, glr, wdec_pad, gla_b_dec[l][:, None, :],
                          jnp.tile(gla_norm_g[l], GLA_HEADS)[None])

        z, x0 = _hy_pre_call(hy, hy_conv_w[l], hy_conv_b[l][None])
        w1p = jnp.pad(hy_w1[l], ((0, LANES - HY_EMB), (0, 0)))
        filt = _hy_filter_call(feat, w1p, hy_b1[l][None], hy_w2[l], hy_b2[l][:, None, :], hy_w3[l],
                               hy_freq[l][None], deltas, seq)
        ft = filt.T.reshape(hw, 2 * nb, HY_BLOCK)
        zt = jnp.transpose(z.reshape(b, nb, HY_BLOCK, hw), (3, 1, 0, 2))
        zt = jnp.pad(zt, ((0, 0), (0, 0), (0, bp - b), (0, 0))).reshape(hw, nb * bp, HY_BLOCK)
        yt = _hy_conv_call(zt, ft, nb, bp).reshape(hw, nb, bp, HY_BLOCK)[:, :, :b]
        yc = jnp.transpose(yt, (2, 1, 3, 0)).reshape(b, seq, hw)

        rw_full = jnp.pad(router_w[l], ((0, 0), (0, LANES - n_exp)))
        rw_hi = rw_full.astype(BF16)
        rw_pad = jnp.stack([rw_hi, (rw_full - rw_hi.astype(F32)).astype(BF16)])
        x, h2, aff = _outproj_call(x, o_gla, o_att, yc, z, x0, hy_skip[l][None], w_out[l].astype(BF16), mod[l],
                                   mix_post_g[l][None], ffn_pre_g[l][None], rw_pad, n_exp)

        aff_t = jnp.swapaxes(aff[..., :n_exp], 1, 2)
        pos_rows, cum = _route_call(aff_t, cap)
        lo, hi = _chunk_ranges(cum, cap, min(EC_TOKEN_CHUNK, seq))
        aff_rows = aff_t.reshape(b * n_exp, 1, seq)
        xgt, gc = _dispatch_call(lo, hi, pos_rows, aff_rows, h2, n_exp, cap)
        yt = _ffn_call(xgt, gc, exp_w_gate, exp_w_up, exp_w_down, l)
        acc_t = _combine_call(lo, hi, pos_rows, yt, b, seq, cap)
        x = _ffn_residual_call(acc_t, x, mod[l], ffn_post_g[l][None])
    return x
```

```python
import functools
import math

import jax
import jax.numpy as jnp
import numpy as np
from jax import lax
from jax.experimental import pallas as pl
from jax.experimental.pallas import tpu as pltpu

F32 = jnp.float32
BF16 = jnp.bfloat16
HI = lax.Precision.HIGHEST
SDS = jax.ShapeDtypeStruct

EPS = 1e-6
GRID_W = 64
HEAD_DIM = 64
GLA_HEADS = 4
GLA_DK = 32
GLA_DV = 64
GLA_QK = GLA_HEADS * GLA_DK
GLA_WIDTH = GLA_HEADS * GLA_DV
GLA_LOWRANK = 16
GLA_GATE_NORM = 16.0
GLA_BLOCK = 16
GLA_STEP_UNROLL = 4
ATT_HEADS = 8
ATT_KV_HEADS = 2
ATT_GROUP = ATT_HEADS // ATT_KV_HEADS
ATT_WIDTH = ATT_HEADS * HEAD_DIM
ATT_KV = ATT_KV_HEADS * HEAD_DIM
ATT_TK = 256
ATT_ONES_ROWS = 16
ROPE_THETA = 10000.0
ROPE_HALF = HEAD_DIM // 2
HY_EMB = 33
HY_BANDS = (HY_EMB - 1) // 2
HY_TARGET = 1e-2
HY_FAST = 0.3
HY_SLOW = 1.5
HY_MIN_DECAY = math.log(HY_TARGET) / HY_SLOW
HY_MAX_DECAY = math.log(HY_TARGET) / HY_FAST
HY_BLOCK = 128
N_EXPERTS = 16
EC_FACTOR = 2
ROUTE_BISECTIONS = 160
EC_SLOT_BLOCK = 256
EC_TOKEN_CHUNK = 512
LANES = 128
VMEM_LIMIT = 56 * 1024 * 1024


def _params(*sem):
    return pltpu.CompilerParams(dimension_semantics=sem, vmem_limit_bytes=VMEM_LIMIT)


def _silu(x):
    return x * (1.0 / (1.0 + jnp.exp(-x)))


def _rms(x):
    return x * lax.rsqrt(jnp.mean(x * x, axis=-1, keepdims=True) + EPS)


def _ada_kernel(c_ref, w_ref, b_ref, o_ref):
    cond = _silu(c_ref[...])
    o_ref[0] = jnp.dot(cond, w_ref[0], precision=HI, preferred_element_type=F32) + b_ref[0]


def _ada_call(c, ada_w, ada_b):
    depth, d, n6 = ada_w.shape
    b = c.shape[0]
    tn = 1024
    return pl.pallas_call(
        _ada_kernel,
        grid=(depth, n6 // tn),
        in_specs=[
            pl.BlockSpec((b, d), lambda l, j: (0, 0)),
            pl.BlockSpec((1, d, tn), lambda l, j: (l, 0, j)),
            pl.BlockSpec((1, 1, tn), lambda l, j: (l, 0, j)),
        ],
        out_specs=pl.BlockSpec((1, b, tn), lambda l, j: (l, 0, j)),
        out_shape=SDS((depth, b, n6), F32),
        compiler_params=_params("arbitrary", "arbitrary"),
        name="ada_mod",
    )(c, ada_w, ada_b.reshape(depth, 1, n6))


_C_GQK = (0, 256)
_C_GV = (256, 512)
_C_GG = (512, 768)
_C_GLR = (768, 896)
_C_AQ = (896, 1408)
_C_AK = (1408, 1536)
_C_AV = (1536, 1664)
_C_HY = (1664, 2432)
_NPACK = 2432


def _head_mean_matrix():
    r = lax.broadcasted_iota(jnp.int32, (LANES, LANES), 0) // HEAD_DIM
    c = lax.broadcasted_iota(jnp.int32, (LANES, LANES), 1) // HEAD_DIM
    return jnp.where(r == c, 1.0 / HEAD_DIM, 0.0).astype(BF16)


def _norm_rope(a, gain, cos, sin_signed, gmat, first_half):
    ms = jnp.dot((a * a).astype(BF16), gmat, preferred_element_type=F32)
    xn = a * lax.rsqrt(ms + EPS) * gain
    rot = jnp.where(first_half, pltpu.roll(xn, LANES - ROPE_HALF // 2, 1), pltpu.roll(xn, ROPE_HALF // 2, 1))
    return xn * cos + rot * sin_signed


def _inproj_kernel(x_ref, mod_ref, g_ref, w_ref, cos_ref, sin_ref, qg_ref, kg_ref,
                   gqk_ref, gv_ref, gg_ref, glr_ref, q_ref, k_ref, vt_ref, hy_ref, *, tk):
    x = x_ref[0]
    sh = mod_ref[0, 0:1, :]
    sc = mod_ref[0, 1:2, :]
    h = (_rms(x) * g_ref[...] * (1.0 + sc) + sh).astype(BF16)

    def seg(c):
        return jnp.dot(h, w_ref[:, c[0]:c[1]], preferred_element_type=F32)

    gqk_ref[0] = seg(_C_GQK)
    gv_ref[0] = seg(_C_GV)
    gg_ref[0] = seg(_C_GG)
    glr_ref[0] = seg(_C_GLR)
    hy_ref[0] = seg(_C_HY)
    av = seg(_C_AV)
    for ci in range(av.shape[0] // tk):
        vt_ref[0, ci] = av[ci * tk:(ci + 1) * tk, :].T.astype(BF16)

    gmat = _head_mean_matrix()
    cos = cos_ref[...]
    sin_signed = sin_ref[...]
    lane = lax.broadcasted_iota(jnp.int32, cos.shape, 1)
    first_half = (lane % ROPE_HALF) < (ROPE_HALF // 2)
    aq = seg(_C_AQ)
    scale = HEAD_DIM ** -0.5
    for j in range(ATT_WIDTH // LANES):
        qj = _norm_rope(aq[:, j * LANES:(j + 1) * LANES], qg_ref[...], cos, sin_signed, gmat, first_half)
        q_ref[0, :, j * LANES:(j + 1) * LANES] = (qj * scale).astype(BF16)
    k_ref[0] = _norm_rope(seg(_C_AK), kg_ref[...], cos, sin_signed, gmat, first_half).astype(BF16)


def _inproj_call(x, mod_l, g, w_pack, cos2, sin2, qg2, kg2):
    b, l, d = x.shape
    tm = min(512, l)
    tk = min(ATT_TK, l)
    row = lambda bi, i: (bi, i, 0)
    full2 = lambda bi, i: (0, 0)
    rows_out = [
        (SDS((b, l, 256), F32), 256), (SDS((b, l, 256), F32), 256), (SDS((b, l, 256), F32), 256),
        (SDS((b, l, 128), F32), 128), (SDS((b, l, ATT_WIDTH), BF16), ATT_WIDTH),
        (SDS((b, l, ATT_KV), BF16), ATT_KV),
    ]
    out_specs = [pl.BlockSpec((1, tm, w), row) for _, w in rows_out]
    out_shape = [s for s, _ in rows_out]
    out_specs.append(pl.BlockSpec((1, tm // tk, ATT_KV, tk), lambda bi, i: (bi, i, 0, 0)))
    out_shape.append(SDS((b, l // tk, ATT_KV, tk), BF16))
    out_specs.append(pl.BlockSpec((1, tm, 768), row))
    out_shape.append(SDS((b, l, 768), F32))
    return pl.pallas_call(
        functools.partial(_inproj_kernel, tk=tk),
        grid=(b, l // tm),
        in_specs=[
            pl.BlockSpec((1, tm, d), row),
            pl.BlockSpec((1, 6, d), lambda bi, i: (bi, 0, 0)),
            pl.BlockSpec((1, d), full2),
            pl.BlockSpec((d, _NPACK), full2),
            pl.BlockSpec((tm, LANES), lambda bi, i: (i, 0)),
            pl.BlockSpec((tm, LANES), lambda bi, i: (i, 0)),
            pl.BlockSpec((1, LANES), full2),
            pl.BlockSpec((1, LANES), full2),
        ],
        out_specs=out_specs,
        out_shape=out_shape,
        compiler_params=_params("parallel", "parallel"),
        name="in_proj",
    )(x, mod_l, g, w_pack, cos2, sin2, qg2, kg2)


def _attn_kernel(q_ref, k_ref, vt_ref, o_ref, acc_ref, s_ref, p_ref, *, tq):
    nchunk = vt_ref.shape[1]
    tk = vt_ref.shape[3]
    lane = lax.broadcasted_iota(jnp.int32, (tq, LANES), 1)
    lower = lane < HEAD_DIM
    ones_rows = ATT_ONES_ROWS
    vrows = HEAD_DIM + ones_rows
    row = lax.broadcasted_iota(jnp.int32, (vrows, tk), 0)
    q = q_ref[0].astype(F32)
    streams = []
    qts = []
    for j in range(ATT_KV_HEADS):
        keep = lower if j == 0 else jnp.logical_not(lower)
        for hp in range(ATT_GROUP // 2):
            cb = j * (ATT_GROUP // 2) + hp
            qc = q[:, cb * LANES:(cb + 1) * LANES]
            qsw = pltpu.roll(qc, HEAD_DIM, 1)
            first, second = (qc, qsw) if j == 0 else (qsw, qc)
            cols = [jnp.where(keep, first, 0.0).T, jnp.where(keep, second, 0.0).T]
            qts.append(jnp.concatenate(cols, axis=1).astype(BF16))
            streams.append((j, cb))
    qt_all = jnp.concatenate(qts, axis=1)
    half = ATT_GROUP * tq
    acc_ref[...] = jnp.zeros(acc_ref.shape, F32)
    p_ref[1] = jnp.zeros(p_ref.shape[1:], BF16)
    ones = jnp.ones((vrows, tk), BF16)

    def scores(c):
        kc = k_ref[0, pl.ds(pl.multiple_of(c * tk, tk), tk), :]
        return jnp.dot(kc, qt_all, preferred_element_type=F32)

    def add_values(c, slot, alpha):
        vt = vt_ref[0, c]
        for j in range(ATT_KV_HEADS):
            r0 = j * (LANES - vrows)
            vm = jnp.where((row < HEAD_DIM) if j == 0 else (row >= ones_rows), vt[r0:r0 + vrows, :], ones)
            pj = p_ref[slot, :, j * half:(j + 1) * half]
            acc_ref[j] = acc_ref[j] * alpha[:, j * half:(j + 1) * half] + jnp.dot(vm, pj, preferred_element_type=F32)

    s_ref[0] = scores(0)

    def step(c, slot, carry):
        m_prev, alpha_prev = carry
        s_ref[1 - slot] = scores(jnp.minimum(c + 1, nchunk - 1))
        add_values(jnp.maximum(c - 1, 0), 1 - slot, alpha_prev)
        st = s_ref[slot]
        m_new = jnp.maximum(m_prev, jnp.max(st, axis=0, keepdims=True))
        p_ref[slot] = jnp.exp((st - m_new).astype(BF16))
        return m_new, jnp.exp(m_prev - m_new)

    def body(i, carry):
        return step(2 * i + 1, 1, step(2 * i, 0, carry))

    assert nchunk % 2 == 0
    init = (jnp.full((1, 2 * half), -jnp.inf, F32), jnp.ones((1, 2 * half), F32))
    _, alpha = lax.fori_loop(0, nchunk // 2, body, init)
    add_values(nchunk - 1, 1, alpha)

    for j, cb in streams:
        hp = cb % (ATT_GROUP // 2)
        acc = acc_ref[j][:, hp * 2 * tq:(hp + 1) * 2 * tq]
        d0 = j * ones_rows
        denom = acc[(1 - j) * HEAD_DIM:(1 - j) * HEAD_DIM + 1, :]
        data = acc[d0:d0 + HEAD_DIM, :] / denom
        blk = jnp.concatenate([data[:, :tq], data[:, tq:]], axis=0)
        o_ref[0, :, cb * LANES:(cb + 1) * LANES] = blk.T.astype(BF16)


def _attn_call(q, k, vt):
    b, l, _ = q.shape
    tq = min(512, l)
    nchunk, tk = vt.shape[1], vt.shape[3]
    nq = ATT_HEADS * tq
    return pl.pallas_call(
        functools.partial(_attn_kernel, tq=tq),
        grid=(b, l // tq),
        in_specs=[
            pl.BlockSpec((1, tq, ATT_WIDTH), lambda bi, i: (bi, i, 0)),
            pl.BlockSpec((1, l, ATT_KV), lambda bi, i: (bi, 0, 0)),
            pl.BlockSpec((1, nchunk, ATT_KV, tk), lambda bi, i: (bi, 0, 0, 0)),
        ],
        out_specs=pl.BlockSpec((1, tq, ATT_WIDTH), lambda bi, i: (bi, i, 0)),
        out_shape=SDS((b, l, ATT_WIDTH), BF16),
        scratch_shapes=[pltpu.VMEM((ATT_KV_HEADS, HEAD_DIM + ATT_ONES_ROWS, nq // ATT_KV_HEADS), F32),
                        pltpu.VMEM((2, tk, nq), F32), pltpu.VMEM((2, tk, nq), BF16)],
        compiler_params=_params("parallel", "parallel"),
        name="gqa_attention",
    )(q, k, vt)


def _log_sigmoid(x):
    return jnp.minimum(x, 0.0) - jnp.log(1.0 + jnp.exp(-jnp.abs(x)))


def _gla_kernel(qk_ref, v_ref, gg_ref, lr_ref, wdec_ref, bdec_ref, ng_ref, o_ref,
                qs_ref, ks_ref, a_ref, od_ref, *, rt):
    seq = qk_ref.shape[1]
    cb = GLA_BLOCK
    rowmod = lax.broadcasted_iota(jnp.int32, (rt, GLA_QK), 0) % cb
    hd = lax.broadcasted_iota(jnp.int32, (GLA_QK, GLA_WIDTH), 0) // GLA_DK
    hc = lax.broadcasted_iota(jnp.int32, (GLA_QK, GLA_WIDTH), 1) // GLA_DV
    expand = jnp.where(hd == hc, 1.0, 0.0).astype(BF16)
    scale = GLA_DK ** -0.5

    def prep(t, _):
        r0 = pl.multiple_of(t * rt, rt)
        rows = pl.ds(r0, rt)
        q = qk_ref[0, rows, 0:GLA_QK] * scale
        k = qk_ref[0, rows, GLA_QK:2 * GLA_QK]
        v = v_ref[0, rows, :]
        lr = lr_ref[0, rows, :]
        od = jnp.zeros((rt, GLA_WIDTH), F32)
        for direction in range(2):
            la = _log_sigmoid(jnp.dot(lr, wdec_ref[direction], precision=HI, preferred_element_type=F32)
                              + bdec_ref[direction]) * (1.0 / GLA_GATE_NORM)
            pre = la
            suf = la
            s = 1
            while s < cb:
                pre = pre + jnp.where(rowmod >= s, pltpu.roll(pre, s, 0), 0.0)
                suf = suf + jnp.where(rowmod < cb - s, pltpu.roll(suf, rt - s, 0), 0.0)
                s *= 2
            tot = pre + suf - la
            cum = pre if direction == 0 else suf
            rem = (suf if direction == 0 else pre) - la
            qs_ref[direction, rows, :] = q * jnp.exp(cum)
            ks_ref[direction, rows, :] = k * jnp.exp(rem)
            a_ref[direction, rows, :] = jnp.exp(tot)
            for delta in range(cb):
                if delta == 0:
                    a_mat = q * k
                    vj = v
                else:
                    if direction == 0:
                        sh = delta
                        valid = rowmod >= delta
                    else:
                        sh = rt - delta
                        valid = rowmod < cb - delta
                    kj = pltpu.roll(k, sh, 0)
                    cj = pltpu.roll(cum, sh, 0)
                    vj = pltpu.roll(v, sh, 0)
                    a_mat = jnp.where(valid, q * kj * jnp.exp(cum - cj), 0.0)
                se = jnp.dot(a_mat.astype(BF16), expand, preferred_element_type=F32)
                od = od + se * vj
        od_ref[rows, :] = od
        return 0

    lax.fori_loop(0, seq // rt, prep, 0)

    sr = lax.broadcasted_iota(jnp.int32, (GLA_WIDTH, GLA_QK), 0) // GLA_DV
    scol = lax.broadcasted_iota(jnp.int32, (GLA_WIDTH, GLA_QK), 1) // GLA_DK
    same_head = sr == scol
    nblk = seq // cb
    unroll = GLA_STEP_UNROLL
    span = unroll * cb

    def step(i, states):
        states = list(states)
        for direction in range(2):
            base = i * span if direction == 0 else seq - (i + 1) * span
            rows = pl.ds(pl.multiple_of(base, span), span)
            qb = qs_ref[direction, rows, :].astype(BF16)
            kb = ks_ref[direction, rows, :].astype(BF16)
            vb = v_ref[0, rows, :].astype(BF16)
            ab = a_ref[direction, rows, :]
            st = states[direction]
            outs = [None] * unroll
            for u in (range(unroll) if direction == 0 else range(unroll - 1, -1, -1)):
                sl = slice(u * cb, (u + 1) * cb)
                outs[u] = lax.dot_general(qb[sl], st.astype(BF16), (((1,), (1,)), ((), ())),
                                          preferred_element_type=F32)
                kv = lax.dot_general(vb[sl], kb[sl], (((0,), (0,)), ((), ())), preferred_element_type=F32)
                st = st * ab[u * cb:u * cb + 1, :] + jnp.where(same_head, kv, 0.0)
            od_ref[rows, :] += jnp.concatenate(outs, axis=0)
            states[direction] = st
        return tuple(states)

    zero_state = jnp.zeros((GLA_WIDTH, GLA_QK), F32)
    lax.fori_loop(0, nblk // unroll, step, (zero_state, zero_state))

    gr = lax.broadcasted_iota(jnp.int32, (GLA_WIDTH, GLA_WIDTH), 0) // GLA_DV
    gc = lax.broadcasted_iota(jnp.int32, (GLA_WIDTH, GLA_WIDTH), 1) // GLA_DV
    gmat = jnp.where(gr == gc, 1.0 / GLA_DV, 0.0).astype(BF16)

    def fin(t, _):
        rows = pl.ds(pl.multiple_of(t * rt, rt), rt)
        o = od_ref[rows, :]
        ms = jnp.dot((o * o).astype(BF16), gmat, preferred_element_type=F32)
        o = o * lax.rsqrt(ms + EPS) * ng_ref[...]
        o_ref[0, rows, :] = (o * _silu(gg_ref[0, rows, :])).astype(BF16)
        return 0

    lax.fori_loop(0, seq // rt, fin, 0)


def _gla_call(gqk, gv, gg, glr, wdec_pad, bdec, ng4):
    b, l, _ = gqk.shape
    rt = min(256, l)
    once = dict(pipeline_mode=pl.Buffered(1))
    bmap = lambda bi: (bi, 0, 0)
    return pl.pallas_call(
        functools.partial(_gla_kernel, rt=rt),
        grid=(b,),
        in_specs=[
            pl.BlockSpec((1, l, 2 * GLA_QK), bmap, **once),
            pl.BlockSpec((1, l, GLA_WIDTH), bmap, **once),
            pl.BlockSpec((1, l, GLA_WIDTH), bmap, **once),
            pl.BlockSpec((1, l, LANES), bmap, **once),
            pl.BlockSpec((2, LANES, GLA_QK), lambda bi: (0, 0, 0)),
            pl.BlockSpec((2, 1, GLA_QK), lambda bi: (0, 0, 0)),
            pl.BlockSpec((1, GLA_WIDTH), lambda bi: (0, 0)),
        ],
        out_specs=pl.BlockSpec((1, l, GLA_WIDTH), bmap),
        out_shape=SDS((b, l, GLA_WIDTH), BF16),
        scratch_shapes=[
            pltpu.VMEM((2, l, GLA_QK), F32),
            pltpu.VMEM((2, l, GLA_QK), F32),
            pltpu.VMEM((2, l, GLA_QK), F32),
            pltpu.VMEM((l, GLA_WIDTH), F32),
        ],
        compiler_params=_params("parallel"),
        name="gla",
    )(gqk, gv, gg, glr, wdec_pad, bdec, ng4)


def _hy_pre_kernel(x0_ref, x1_ref, v_ref, w0_ref, w1_ref, wv_ref, b0_ref, b1_ref, bv_ref, z_ref, x0o_ref, *, rt):
    seq = x0_ref.shape[1]
    rowid = lax.broadcasted_iota(jnp.int32, (rt, LANES), 0)

    def conv(ref, w_ref, b_ref, r0):
        cur = ref[0, pl.ds(r0, rt), :]
        prev = ref[0, pl.ds(pl.multiple_of(jnp.maximum(r0 - 8, 0), 8), 8), :][7:8, :]
        nxt = ref[0, pl.ds(pl.multiple_of(jnp.minimum(r0 + rt, seq - 8), 8), 8), :][0:1, :]
        prev = jnp.where(r0 > 0, prev, 0.0)
        nxt = jnp.where(r0 + rt < seq, nxt, 0.0)
        up = jnp.where(rowid == 0, prev, pltpu.roll(cur, 1, 0))
        down = jnp.where(rowid == rt - 1, nxt, pltpu.roll(cur, rt - 1, 0))
        return up * w_ref[0:1, :] + cur * w_ref[1:2, :] + down * w_ref[2:3, :] + b_ref[...]

    def body(t, _):
        r0 = pl.multiple_of(t * rt, rt)
        rows = pl.ds(r0, rt)
        x0o_ref[0, rows, :] = conv(x0_ref, w0_ref, b0_ref, r0)
        z_ref[0, rows, :] = conv(v_ref, wv_ref, bv_ref, r0) * conv(x1_ref, w1_ref, b1_ref, r0)
        return 0

    lax.fori_loop(0, seq // rt, body, 0)


def _hy_pre_call(hy, conv_w, conv_b):
    b, l, w3 = hy.shape
    hw = w3 // 3
    nj = hw // LANES
    rt = min(512, l)
    xs = lambda off: pl.BlockSpec((1, l, LANES), lambda bi, j: (bi, 0, off * nj + j))
    ws = lambda off: pl.BlockSpec((3, LANES), lambda bi, j: (0, off * nj + j))
    bs = lambda off: pl.BlockSpec((1, LANES), lambda bi, j: (0, off * nj + j))
    out = pl.BlockSpec((1, l, LANES), lambda bi, j: (bi, 0, j))
    return pl.pallas_call(
        functools.partial(_hy_pre_kernel, rt=rt),
        grid=(b, nj),
        in_specs=[xs(0), xs(1), xs(2), ws(0), ws(1), ws(2), bs(0), bs(1), bs(2)],
        out_specs=[out, out],
        out_shape=[SDS((b, l, hw), F32), SDS((b, l, hw), F32)],
        compiler_params=_params("parallel", "parallel"),
        name="hyena_short_conv",
    )(hy, hy, hy, conv_w, conv_w, conv_w, conv_b, conv_b, conv_b)


def _hy_filter_kernel(z_ref, w1_ref, b1_ref, w2_ref, b2_ref, w3_ref, fr_ref, dl_ref, o_ref, *, seq, tr):
    z = z_ref[...]
    fr = fr_ref[...]
    h = jnp.sin(fr * (jnp.dot(z, w1_ref[...], precision=HI, preferred_element_type=F32) + b1_ref[...]))
    for i in range(w2_ref.shape[0]):
        h = jnp.sin(fr * (jnp.dot(h, w2_ref[i], precision=HI, preferred_element_type=F32) + b2_ref[i]))
    hw = jnp.dot(h, w3_ref[...], precision=HI, preferred_element_type=F32)
    c = hw.shape[1] // 2
    window = jnp.exp(-z[:, 0:1] * dl_ref[...])
    m = pl.program_id(0) * tr + lax.broadcasted_iota(jnp.int32, (tr, c), 0)
    tap = jnp.where(m >= seq, hw[:, :c], hw[:, c:]) * window
    o_ref[...] = jnp.where(m == 0, 0.0, tap)


def _hy_filter_call(feat, w1p, b1, w2, b2, w3, freq, deltas, seq):
    n2, _ = feat.shape
    od = w1p.shape[1]
    c2 = w3.shape[1]
    tr = min(1024, n2)
    full = lambda *shape: pl.BlockSpec(shape, lambda i: (0,) * len(shape))
    return pl.pallas_call(
        functools.partial(_hy_filter_kernel, seq=seq, tr=tr),
        grid=(n2 // tr,),
        in_specs=[
            pl.BlockSpec((tr, LANES), lambda i: (i, 0)),
            full(LANES, od), full(1, od), full(w2.shape[0], od, od), full(w2.shape[0], 1, od),
            full(od, c2), full(1, od), full(1, c2 // 2),
        ],
        out_specs=pl.BlockSpec((tr, c2 // 2), lambda i: (i, 0)),
        out_shape=SDS((n2, c2 // 2), F32),
        compiler_params=_params("parallel"),
        name="hyena_filter",
    )(feat, w1p, b1, w2, b2, w3, freq, deltas)


def _hy_conv_kernel(z_ref, f_ref, y_ref, acc_ref, *, nb, bp):
    cg = z_ref.shape[0]
    p = HY_BLOCK
    ii = lax.broadcasted_iota(jnp.int32, (p, p), 0)
    jj = lax.broadcasted_iota(jnp.int32, (p, p), 1)
    upper = jj >= ii

    rows = nb * bp
    sub = 8
    shifts = sorted({(bp * d) % sub for d in range(-(nb - 1), nb)})

    def chan(ci, _):
        pad = jnp.zeros((sub, p), F32)
        zfull = jnp.concatenate([pad, z_ref[ci], pad], axis=0)
        zs = {m: (zfull if m == 0 else pltpu.roll(zfull, m, 0)) for m in shifts}
        acc_ref[...] = jnp.zeros(acc_ref.shape, F32)

        def skew(k):
            taps = jnp.broadcast_to(f_ref[ci, k:k + 1, :], (p, p))
            return pltpu.roll(taps, 0, 1, stride=1, stride_axis=0).astype(BF16)

        hi = skew(0)
        for k in range(2 * nb - 1):
            delta = k - (nb - 1)
            lo, hi = hi, skew(k + 1)
            w = jnp.where(upper, hi, lo)
            s = bp * delta
            o0 = max(0, s) // sub * sub
            o1 = -(-min(rows, rows + s) // sub) * sub
            m = s % sub
            src = o0 - s + sub + m
            lhs = zs[m][src:src + (o1 - o0), :].astype(BF16)
            acc_ref[o0:o1, :] += jnp.dot(lhs, w, preferred_element_type=F32)
        y_ref[ci] = acc_ref[...]
        return 0

    lax.fori_loop(0, cg, chan, 0)


def _hy_conv_call(zt, ft, nb, bp):
    c, rows, p = zt.shape
    cg = 8
    return pl.pallas_call(
        functools.partial(_hy_conv_kernel, nb=nb, bp=bp),
        grid=(c // cg,),
        in_specs=[
            pl.BlockSpec((cg, rows, p), lambda g: (g, 0, 0)),
            pl.BlockSpec((cg, 2 * nb, p), lambda g: (g, 0, 0)),
        ],
        out_specs=pl.BlockSpec((cg, rows, p), lambda g: (g, 0, 0)),
        out_shape=SDS((c, rows, p), F32),
        scratch_shapes=[pltpu.VMEM((rows, p), F32)],
        compiler_params=_params("parallel"),
        name="hyena_long_conv",
    )(zt, ft)


def _outproj_kernel(x_ref, ogla_ref, oatt_ref, yc_ref, z_ref, x0_ref, skip_ref, w_ref, mod_ref,
                    gpost_ref, gpre_ref, rw_ref, xo_ref, h2_ref, aff_ref, *, n_exp):
    ohy = (x0_ref[0] * (yc_ref[0] + z_ref[0] * skip_ref[...])).astype(BF16)
    g0 = GLA_WIDTH
    g1 = GLA_WIDTH + ATT_WIDTH
    y = jnp.dot(ogla_ref[0], w_ref[0:g0, :], preferred_element_type=F32)
    y = y + jnp.dot(oatt_ref[0], w_ref[g0:g1, :], preferred_element_type=F32)
    y = y + jnp.dot(ohy, w_ref[g1:, :], preferred_element_type=F32)
    g_m = mod_ref[0, 2:3, :]
    sh_f = mod_ref[0, 3:4, :]
    sc_f = mod_ref[0, 4:5, :]
    x1 = x_ref[0] + g_m * (_rms(y) * gpost_ref[...])
    xo_ref[0] = x1
    h2 = _rms(x1) * gpre_ref[...] * (1.0 + sc_f) + sh_f
    h2_ref[0] = h2.T.astype(BF16)
    h_hi = h2.astype(BF16)
    h_lo = (h2 - h_hi.astype(F32)).astype(BF16)
    logits = (jnp.dot(h_hi, rw_ref[0], preferred_element_type=F32)
              + jnp.dot(h_lo, rw_ref[0], preferred_element_type=F32)
              + jnp.dot(h_hi, rw_ref[1], preferred_element_type=F32))
    lane = lax.broadcasted_iota(jnp.int32, logits.shape, 1)
    logits = jnp.where(lane < n_exp, logits, -jnp.inf)
    e = jnp.exp(logits - jnp.max(logits, axis=-1, keepdims=True))
    aff_ref[0] = e / jnp.sum(e, axis=-1, keepdims=True)


def _outproj_call(x, ogla, oatt, yc, z, x0, skip, w_out, mod_l, gpost, gpre, rw_pad, n_exp):
    b, l, d = x.shape
    tm = min(512, l)
    row = lambda bi, i: (bi, i, 0)
    full2 = lambda bi, i: (0, 0)
    hw = yc.shape[-1]
    return pl.pallas_call(
        functools.partial(_outproj_kernel, n_exp=n_exp),
        grid=(b, l // tm),
        in_specs=[
            pl.BlockSpec((1, tm, d), row),
            pl.BlockSpec((1, tm, GLA_WIDTH), row),
            pl.BlockSpec((1, tm, ATT_WIDTH), row),
            pl.BlockSpec((1, tm, hw), row),
            pl.BlockSpec((1, tm, hw), row),
            pl.BlockSpec((1, tm, hw), row),
            pl.BlockSpec((1, hw), full2),
            pl.BlockSpec((d, d), full2),
            pl.BlockSpec((1, 6, d), lambda bi, i: (bi, 0, 0)),
            pl.BlockSpec((1, d), full2),
            pl.BlockSpec((1, d), full2),
            pl.BlockSpec((2, d, LANES), lambda bi, i: (0, 0, 0)),
        ],
        out_specs=[pl.BlockSpec((1, tm, d), row), pl.BlockSpec((1, d, tm), lambda bi, i: (bi, 0, i)),
                   pl.BlockSpec((1, tm, LANES), row)],
        out_shape=[SDS((b, l, d), F32), SDS((b, d, l), BF16), SDS((b, l, LANES), F32)],
        compiler_params=_params("parallel", "parallel"),
        name="out_proj_router",
    )(x, ogla, oatt, yc, z, x0, skip, w_out, mod_l, gpost, gpre, rw_pad)


def _route_kernel(aff_ref, pos_ref, cum_ref, *, cap):
    aff = aff_ref[0]
    n_exp, t = aff.shape

    def bisect(_, bounds):
        lo, hi = bounds
        mid = 0.5 * (lo + hi)
        ok = jnp.sum(jnp.where(aff >= mid, 1.0, 0.0), axis=1, keepdims=True) >= cap
        return jnp.where(ok, mid, lo), jnp.where(ok, hi, mid)

    lo, hi = lax.fori_loop(0, ROUTE_BISECTIONS, bisect,
                           (jnp.zeros((n_exp, 1), F32), jnp.full((n_exp, 1), 2.0, F32)))
    need = cap - jnp.sum(jnp.where(aff >= hi, 1.0, 0.0), axis=1, keepdims=True)
    r = lax.broadcasted_iota(jnp.int32, (LANES, LANES), 0)
    c = lax.broadcasted_iota(jnp.int32, (LANES, LANES), 1)
    tri = jnp.where(r <= c, 1.0, 0.0).astype(BF16)
    eq_run = jnp.zeros((n_exp, 1), F32)
    sel_run = jnp.zeros((n_exp, 1), F32)
    runs = []
    for j in range(t // LANES):
        sl = slice(j * LANES, (j + 1) * LANES)
        aj = aff[:, sl]
        eq_j = jnp.where(aj >= hi, 0.0, jnp.where(aj >= lo, 1.0, 0.0))
        eq_cum = jnp.dot(eq_j.astype(BF16), tri, preferred_element_type=F32) + eq_run
        sel_j = jnp.where(aj >= hi, 1.0, jnp.where(eq_cum <= need, eq_j, 0.0))
        sel_cum = jnp.dot(sel_j.astype(BF16), tri, preferred_element_type=F32) + sel_run
        pos_ref[0, :, sl] = jnp.where(sel_j > 0.0, sel_cum - 1.0, -1.0).astype(jnp.int32)
        eq_run = eq_cum[:, LANES - 1:LANES]
        sel_run = sel_cum[:, LANES - 1:LANES]
        runs.append(sel_run)
    cum_ref[0] = jnp.concatenate(runs, axis=1).astype(jnp.int32)


def _route_call(aff_t, cap):
    b, n_exp, t = aff_t.shape
    rows = b * n_exp
    pos, cum = pl.pallas_call(
        functools.partial(_route_kernel, cap=cap),
        grid=(1,),
        in_specs=[pl.BlockSpec((1, rows, t), lambda i: (0, 0, 0))],
        out_specs=[pl.BlockSpec((1, rows, t), lambda i: (0, 0, 0)),
                   pl.BlockSpec((1, rows, t // LANES), lambda i: (0, 0, 0))],
        out_shape=[SDS((1, rows, t), jnp.int32), SDS((1, rows, t // LANES), jnp.int32)],
        compiler_params=_params("arbitrary"),
        name="ec_route",
    )(aff_t.reshape(1, rows, t))
    return pos.reshape(rows, 1, t), cum.reshape(rows, t // LANES)


def _chunk_ranges(cum, cap, tc):
    per = tc // LANES
    cum_c = cum[:, per - 1::per]
    first = jnp.arange(0, cap, EC_SLOT_BLOCK, dtype=jnp.int32)
    chunk_of = lambda s: jnp.sum(cum_c[:, None, :] <= s[None, :, None], axis=-1).astype(jnp.int32)
    return chunk_of(first).reshape(-1), chunk_of(first + EC_SLOT_BLOCK - 1).reshape(-1)


def _one_hot(pos_ref, off, tc, first_slot):
    slot = first_slot + lax.broadcasted_iota(jnp.int32, (EC_SLOT_BLOCK, tc), 0)
    return jnp.where(slot == pos_ref[0, :, pl.ds(off, tc)], 1.0, 0.0).astype(BF16)


def _dispatch_kernel(lo_ref, hi_ref, pos_ref, aff_ref, ht_ref, xgt_ref, gc_ref, acc_ref, gacc_ref, *, tc, no_slot):
    nr = pl.num_programs(2)
    r = pl.program_id(2)
    idx = (pl.program_id(0) * pl.num_programs(1) + pl.program_id(1)) * nr + r
    acc_ref[...] = jnp.zeros(acc_ref.shape, F32)
    gacc_ref[...] = jnp.zeros(gacc_ref.shape, F32)
    nt = (((1,), (1,)), ((), ()))

    lo = lo_ref[idx]
    hi = hi_ref[idx]

    def body(i, _):
        c0 = lo + 2 * i
        chunks = ((c0, r * EC_SLOT_BLOCK),
                  (jnp.minimum(c0 + 1, hi), jnp.where(c0 + 1 <= hi, r * EC_SLOT_BLOCK, no_slot)))
        part = gpart = None
        for c, first_slot in chunks:
            off = pl.multiple_of(c * tc, tc)
            sel = _one_hot(pos_ref, off, tc, first_slot)
            d = lax.dot_general(ht_ref[0, :, pl.ds(off, tc)], sel, nt, preferred_element_type=F32)
            g = aff_ref[0, :, pl.ds(off, tc)]
            g1 = g.astype(BF16).astype(F32)
            g2 = (g - g1).astype(BF16).astype(F32)
            g3 = g - g1 - g2
            prow = lax.broadcasted_iota(jnp.int32, (gacc_ref.shape[0], tc), 0)
            pieces = jnp.where(prow == 0, g1, jnp.where(prow == 1, g2, jnp.where(prow == 2, g3, 0.0))).astype(BF16)
            gd = lax.dot_general(pieces, sel, nt, preferred_element_type=F32)
            part = d if part is None else part + d
            gpart = gd if gpart is None else gpart + gd
        acc_ref[...] += part
        gacc_ref[...] += gpart
        return 0

    lax.fori_loop(0, (hi - lo + 2) // 2, body, 0)
    xgt_ref[0] = acc_ref[...].astype(BF16)
    gc_ref[0] = jnp.sum(gacc_ref[...], axis=0, keepdims=True)


def _dispatch_call(lo, hi, pos_rows, aff_rows, h2t, n_exp, cap):
    b, d, t = h2t.shape
    tc = min(EC_TOKEN_CHUNK, t)
    nr = cap // EC_SLOT_BLOCK
    row = lambda bi, e, r, lo_r, hi_r: (bi * n_exp + e, 0, 0)
    out = lambda bi, e, r, lo_r, hi_r: (e, 0, bi * nr + r)
    return pl.pallas_call(
        functools.partial(_dispatch_kernel, tc=tc, no_slot=cap),
        grid_spec=pltpu.PrefetchScalarGridSpec(
            num_scalar_prefetch=2,
            grid=(b, n_exp, nr),
            in_specs=[
                pl.BlockSpec((1, 1, t), row),
                pl.BlockSpec((1, 1, t), row),
                pl.BlockSpec((1, d, t), lambda bi, e, r, lo_r, hi_r: (bi, 0, 0)),
            ],
            out_specs=[pl.BlockSpec((1, d, EC_SLOT_BLOCK), out), pl.BlockSpec((1, 1, EC_SLOT_BLOCK), out)],
            scratch_shapes=[pltpu.VMEM((d, EC_SLOT_BLOCK), F32), pltpu.VMEM((16, EC_SLOT_BLOCK), F32)],
        ),
        out_shape=[SDS((n_exp, d, b * cap), BF16), SDS((n_exp, 1, b * cap), F32)],
        compiler_params=_params("parallel", "arbitrary", "arbitrary"),
        name="ec_dispatch",
    )(lo, hi, pos_rows, aff_rows, h2t)


def _ffn_kernel(xgt_ref, gc_ref, wg_ref, wu_ref, wd_ref, yt_ref, xg_ref, acc_ref, *, mt):
    f = pl.program_id(1)
    m = xg_ref.shape[0]

    @pl.when(f == 0)
    def _():
        acc_ref[...] = jnp.zeros(acc_ref.shape, F32)
        for i in range(m // mt):
            xg_ref[i * mt:(i + 1) * mt, :] = xgt_ref[0, :, i * mt:(i + 1) * mt].T

    wg = wg_ref[0, 0].astype(BF16)
    wu = wu_ref[0, 0].astype(BF16)
    wd = wd_ref[0, 0].astype(BF16)
    for i in range(m // mt):
        rows = slice(i * mt, (i + 1) * mt)
        xb = xg_ref[rows, :]
        a = jnp.dot(xb, wg, preferred_element_type=F32)
        u = jnp.dot(xb, wu, preferred_element_type=F32)
        acc_ref[rows, :] += jnp.dot((_silu(a) * u).astype(BF16), wd, preferred_element_type=F32)

    @pl.when(f == pl.num_programs(1) - 1)
    def _():
        for i in range(m // mt):
            cols = slice(i * mt, (i + 1) * mt)
            yt_ref[0, :, cols] = (acc_ref[cols, :].T * gc_ref[0, :, cols]).astype(BF16)


def _ffn_call(xgt, gc, w_gate, w_up, w_down, layer):
    n_exp, d, m = xgt.shape
    ff = w_gate.shape[3]
    tf = 256
    mt = min(512, m)
    return pl.pallas_call(
        functools.partial(_ffn_kernel, mt=mt),
        grid=(n_exp, ff // tf),
        in_specs=[
            pl.BlockSpec((1, d, m), lambda e, f: (e, 0, 0)),
            pl.BlockSpec((1, 1, m), lambda e, f: (e, 0, 0)),
            pl.BlockSpec((1, 1, d, tf), lambda e, f: (layer, e, 0, f)),
            pl.BlockSpec((1, 1, d, tf), lambda e, f: (layer, e, 0, f)),
            pl.BlockSpec((1, 1, tf, d), lambda e, f: (layer, e, f, 0)),
        ],
        out_specs=pl.BlockSpec((1, d, m), lambda e, f: (e, 0, 0)),
        out_shape=SDS((n_exp, d, m), BF16),
        scratch_shapes=[pltpu.VMEM((m, d), BF16), pltpu.VMEM((m, d), F32)],
        compiler_params=_params("parallel", "arbitrary"),
        name="expert_ffn",
    )(xgt, gc, w_gate, w_up, w_down)


def _combine_kernel(lo_ref, hi_ref, pos_ref, yt_ref, o_ref, *, tc, no_slot):
    ne = pl.num_programs(1)
    nr = pl.num_programs(2)
    e = pl.program_id(1)
    r = pl.program_id(2)
    idx = (pl.program_id(0) * ne + e) * nr + r

    @pl.when(jnp.logical_and(e == 0, r == 0))
    def _():
        o_ref[...] = jnp.zeros(o_ref.shape, F32)

    yb = yt_ref[0]
    lo = lo_ref[idx]
    hi = hi_ref[idx]

    def body(i, _):
        c0 = lo + 2 * i
        chunks = ((c0, r * EC_SLOT_BLOCK),
                  (jnp.minimum(c0 + 1, hi), jnp.where(c0 + 1 <= hi, r * EC_SLOT_BLOCK, no_slot)))
        adds = []
        for c, first_slot in chunks:
            off = pl.multiple_of(c * tc, tc)
            adds.append((off, jnp.dot(yb, _one_hot(pos_ref, off, tc, first_slot), preferred_element_type=F32)))
        for off, contrib in adds:
            o_ref[0, :, pl.ds(off, tc)] += contrib
        return 0

    lax.fori_loop(0, (hi - lo + 2) // 2, body, 0)


def _combine_call(lo, hi, pos_rows, yt, b, t, cap):
    n_exp, d, _ = yt.shape
    tc = min(EC_TOKEN_CHUNK, t)
    nr = cap // EC_SLOT_BLOCK
    return pl.pallas_call(
        functools.partial(_combine_kernel, tc=tc, no_slot=cap),
        grid_spec=pltpu.PrefetchScalarGridSpec(
            num_scalar_prefetch=2,
            grid=(b, n_exp, nr),
            in_specs=[
                pl.BlockSpec((1, 1, t), lambda bi, e, r, lo_r, hi_r: (bi * n_exp + e, 0, 0)),
                pl.BlockSpec((1, d, EC_SLOT_BLOCK), lambda bi, e, r, lo_r, hi_r: (e, 0, bi * nr + r)),
            ],
            out_specs=pl.BlockSpec((1, d, t), lambda bi, e, r, lo_r, hi_r: (bi, 0, 0)),
        ),
        out_shape=SDS((b, d, t), F32),
        compiler_params=_params("parallel", "arbitrary", "arbitrary"),
        name="ec_combine",
    )(lo, hi, pos_rows, yt)


def _ffn_residual_kernel(acc_ref, x_ref, mod_ref, g_ref, o_ref):
    y = acc_ref[0].T
    g_f = mod_ref[0, 5:6, :]
    o_ref[0] = x_ref[0] + g_f * (_rms(y) * g_ref[...])


def _ffn_residual_call(acc_t, x, mod_l, gpost):
    b, t, d = x.shape
    tt = min(512, t)
    return pl.pallas_call(
        _ffn_residual_kernel,
        grid=(b, t // tt),
        in_specs=[
            pl.BlockSpec((1, d, tt), lambda bi, i: (bi, 0, i)),
            pl.BlockSpec((1, tt, d), lambda bi, i: (bi, i, 0)),
            pl.BlockSpec((1, 6, d), lambda bi, i: (bi, 0, 0)),
            pl.BlockSpec((1, d), lambda bi, i: (0, 0)),
        ],
        out_specs=pl.BlockSpec((1, tt, d), lambda bi, i: (bi, i, 0)),
        out_shape=SDS((b, t, d), F32),
        compiler_params=_params("parallel", "parallel"),
        name="ffn_residual",
    )(acc_t, x, mod_l, gpost)


def _rope_tables(seq):
    n_rows = seq // GRID_W
    row = jnp.repeat(jnp.arange(n_rows), GRID_W).astype(F32)
    col = jnp.tile(jnp.arange(GRID_W), n_rows).astype(F32)
    inv = 1.0 / (ROPE_THETA ** (jnp.arange(0, ROPE_HALF, 2, dtype=F32) / ROPE_HALF))
    ar = row[:, None] * inv[None]
    ac = col[:, None] * inv[None]
    ang = jnp.concatenate([ar, ar, ac, ac], axis=-1)
    cos, sin = jnp.cos(ang), jnp.sin(ang)
    first_half = (jnp.arange(HEAD_DIM) % ROPE_HALF) < (ROPE_HALF // 2)
    sin_signed = jnp.where(first_half[None], -sin, sin)
    return jnp.tile(cos, (1, 2)), jnp.tile(sin_signed, (1, 2))


def _filter_features(seq):
    pos = jnp.abs(jnp.arange(2 * seq) - seq).astype(F32)
    t = pos / (seq - 1)
    f = jnp.linspace(1e-4, HY_BANDS - 1, HY_BANDS, dtype=F32)
    ang = (2.0 * math.pi * pos / seq)[:, None] * f[None]
    z = jnp.concatenate([t[:, None], jnp.cos(ang), -jnp.sin(ang)], axis=-1)
    return jnp.pad(z, ((0, 0), (0, LANES - HY_EMB)))


def kernel(x, c, ada_w, ada_b, mix_pre_g, mix_post_g, w_in, gla_w_dec, gla_b_dec, gla_norm_g, q_norm_g, k_norm_g, hy_conv_w, hy_conv_b, hy_w1, hy_b1, hy_w2, hy_b2, hy_w3, hy_freq, hy_skip, w_out, ffn_pre_g, ffn_post_g, router_w, exp_w_gate, exp_w_up, exp_w_down):
    b, seq, d = x.shape
    depth = ada_w.shape[0]
    n_exp = router_w.shape[-1]
    cap = EC_FACTOR * seq // n_exp
    hw = hy_skip.shape[-1]
    nb = seq // HY_BLOCK
    assert seq % HY_BLOCK == 0 and (nb * b) % 8 == 0

    mod = _ada_call(c, ada_w, ada_b).reshape(depth, b, 6, d)
    cos2, sin2 = _rope_tables(seq)
    feat = _filter_features(seq)
    deltas = jnp.abs(jnp.linspace(HY_MIN_DECAY, HY_MAX_DECAY, hw, dtype=F32))[None]

    for l in range(depth):
        w = w_in[l]
        w_pack = jnp.concatenate(
            [w[:, :768], jnp.pad(w[:, 768:800], ((0, 0), (0, LANES - 2 * GLA_LOWRANK))), w[:, 800:]], axis=1
        ).astype(BF16)
        gqk, gv, gg, glr, aq, ak, av, hy = _inproj_call(
            x, mod[l], mix_pre_g[l][None], w_pack, cos2, sin2,
            jnp.tile(q_norm_g[l], 2)[None], jnp.tile(k_norm_g[l], 2)[None])
        o_att = _attn_call(aq, ak, av)

        wdec_pad = jnp.zeros((2, LANES, GLA_QK), F32)
        wdec_pad = wdec_pad.at[0, :GLA_LOWRANK].set(gla_w_dec[l, 0]).at[1, GLA_LOWRANK:2 * GLA_LOWRANK].set(gla_w_dec[l, 1])
        o_gla = _gla_call(gqk, gv, gg, glr, wdec_pad, gla_b_dec[l][:, None, :],
                          jnp.tile(gla_norm_g[l], GLA_HEADS)[None])

        z, x0 = _hy_pre_call(hy, hy_conv_w[l], hy_conv_b[l][None])
        w1p = jnp.pad(hy_w1[l], ((0, LANES - HY_EMB), (0, 0)))
        filt = _hy_filter_call(feat, w1p, hy_b1[l][None], hy_w2[l], hy_b2[l][:, None, :], hy_w3[l],
                               hy_freq[l][None], deltas, seq)
        ft = filt.T.reshape(hw, 2 * nb, HY_BLOCK)
        zt = jnp.transpose(z.reshape(b, nb, HY_BLOCK, hw), (3, 1, 0, 2)).reshape(hw, nb * b, HY_BLOCK)
        yt = _hy_conv_call(zt, ft, nb, b).reshape(hw, nb, b, HY_BLOCK)
        yc = jnp.transpose(yt, (2, 1, 3, 0)).reshape(b, seq, hw)

        rw_full = jnp.pad(router_w[l], ((0, 0), (0, LANES - n_exp)))
        rw_hi = rw_full.astype(BF16)
        rw_pad = jnp.stack([rw_hi, (rw_full - rw_hi.astype(F32)).astype(BF16)])
        x, h2, aff = _outproj_call(x, o_gla, o_att, yc, z, x0, hy_skip[l][None], w_out[l].astype(BF16), mod[l],
                                   mix_post_g[l][None], ffn_pre_g[l][None], rw_pad, n_exp)

        aff_t = jnp.swapaxes(aff[..., :n_exp], 1, 2)
        pos_rows, cum = _route_call(aff_t, cap)
        lo, hi = _chunk_ranges(cum, cap, min(EC_TOKEN_CHUNK, seq))
        aff_rows = aff_t.reshape(b * n_exp, 1, seq)
        xgt, gc = _dispatch_call(lo, hi, pos_rows, aff_rows, h2, n_exp, cap)
        yt = _ffn_call(xgt, gc, exp_w_gate, exp_w_up, exp_w_down, l)
        acc_t = _combine_call(lo, hi, pos_rows, yt, b, seq, cap)
        x = _ffn_residual_call(acc_t, x, mod[l], ffn_post_g[l][None])
    return x
```

```python
import functools
import math

import jax
import jax.numpy as jnp
import numpy as np
from jax import lax
from jax.experimental import pallas as pl
from jax.experimental.pallas import tpu as pltpu

F32 = jnp.float32
BF16 = jnp.bfloat16
HI = lax.Precision.HIGHEST
SDS = jax.ShapeDtypeStruct

EPS = 1e-6
GRID_W = 64
HEAD_DIM = 64
GLA_HEADS = 4
GLA_DK = 32
GLA_DV = 64
GLA_QK = GLA_HEADS * GLA_DK
GLA_WIDTH = GLA_HEADS * GLA_DV
GLA_LOWRANK = 16
GLA_GATE_NORM = 16.0
GLA_BLOCK = 8
GLA_STATE_BLOCK = 64
GLA_STEP_UNROLL = 4
ATT_HEADS = 8
ATT_KV_HEADS = 2
ATT_GROUP = ATT_HEADS // ATT_KV_HEADS
ATT_WIDTH = ATT_HEADS * HEAD_DIM
ATT_KV = ATT_KV_HEADS * HEAD_DIM
ATT_TK = 256
ATT_ONES_ROWS = 16
ROPE_THETA = 10000.0
ROPE_HALF = HEAD_DIM // 2
HY_EMB = 33
HY_BANDS = (HY_EMB - 1) // 2
HY_TARGET = 1e-2
HY_FAST = 0.3
HY_SLOW = 1.5
HY_MIN_DECAY = math.log(HY_TARGET) / HY_SLOW
HY_MAX_DECAY = math.log(HY_TARGET) / HY_FAST
HY_BLOCK = 128
N_EXPERTS = 16
EC_FACTOR = 2
ROUTE_BISECTIONS = 160
EC_SLOT_BLOCK = 256
EC_TOKEN_CHUNK = 512
LANES = 128
VMEM_LIMIT = 56 * 1024 * 1024


def _params(*sem):
    return pltpu.CompilerParams(dimension_semantics=sem, vmem_limit_bytes=VMEM_LIMIT)


def _silu(x):
    return x * (1.0 / (1.0 + jnp.exp(-x)))


def _rms(x):
    return x * lax.rsqrt(jnp.mean(x * x, axis=-1, keepdims=True) + EPS)


def _ada_kernel(c_ref, w_ref, b_ref, o_ref):
    cond = _silu(c_ref[...])
    o_ref[0] = jnp.dot(cond, w_ref[0], precision=HI, preferred_element_type=F32) + b_ref[0]


def _ada_call(c, ada_w, ada_b):
    depth, d, n6 = ada_w.shape
    b = c.shape[0]
    tn = 1024
    return pl.pallas_call(
        _ada_kernel,
        grid=(depth, n6 // tn),
        in_specs=[
            pl.BlockSpec((b, d), lambda l, j: (0, 0)),
            pl.BlockSpec((1, d, tn), lambda l, j: (l, 0, j)),
            pl.BlockSpec((1, 1, tn), lambda l, j: (l, 0, j)),
        ],
        out_specs=pl.BlockSpec((1, b, tn), lambda l, j: (l, 0, j)),
        out_shape=SDS((depth, b, n6), F32),
        compiler_params=_params("arbitrary", "arbitrary"),
        name="ada_mod",
    )(c, ada_w, ada_b.reshape(depth, 1, n6))


_C_GQK = (0, 256)
_C_GV = (256, 512)
_C_GG = (512, 768)
_C_GLR = (768, 896)
_C_AQ = (896, 1408)
_C_AK = (1408, 1536)
_C_AV = (1536, 1664)
_C_HY = (1664, 2432)
_NPACK = 2432


def _head_mean_matrix():
    r = lax.broadcasted_iota(jnp.int32, (LANES, LANES), 0) // HEAD_DIM
    c = lax.broadcasted_iota(jnp.int32, (LANES, LANES), 1) // HEAD_DIM
    return jnp.where(r == c, 1.0 / HEAD_DIM, 0.0).astype(BF16)


def _norm_rope(a, gain, cos, sin_signed, gmat, first_half):
    ms = jnp.dot((a * a).astype(BF16), gmat, preferred_element_type=F32)
    xn = a * lax.rsqrt(ms + EPS) * gain
    rot = jnp.where(first_half, pltpu.roll(xn, LANES - ROPE_HALF // 2, 1), pltpu.roll(xn, ROPE_HALF // 2, 1))
    return xn * cos + rot * sin_signed


def _inproj_kernel(x_ref, mod_ref, g_ref, w_ref, cos_ref, sin_ref, qg_ref, kg_ref,
                   gqk_ref, gv_ref, gg_ref, glr_ref, q_ref, k_ref, vt_ref, hy_ref, *, tk):
    x = x_ref[0]
    sh = mod_ref[0, 0:1, :]
    sc = mod_ref[0, 1:2, :]
    h = (_rms(x) * g_ref[...] * (1.0 + sc) + sh).astype(BF16)

    def seg(c):
        return jnp.dot(h, w_ref[:, c[0]:c[1]], preferred_element_type=F32)

    gqk_ref[0] = seg(_C_GQK)
    gv_ref[0] = seg(_C_GV)
    gg_ref[0] = seg(_C_GG)
    glr_ref[0] = seg(_C_GLR)
    hy_ref[0] = seg(_C_HY)
    av = seg(_C_AV)
    for ci in range(av.shape[0] // tk):
        vt_ref[0, ci] = av[ci * tk:(ci + 1) * tk, :].T.astype(BF16)

    gmat = _head_mean_matrix()
    cos = cos_ref[...]
    sin_signed = sin_ref[...]
    lane = lax.broadcasted_iota(jnp.int32, cos.shape, 1)
    first_half = (lane % ROPE_HALF) < (ROPE_HALF // 2)
    aq = seg(_C_AQ)
    scale = HEAD_DIM ** -0.5
    for j in range(ATT_WIDTH // LANES):
        qj = _norm_rope(aq[:, j * LANES:(j + 1) * LANES], qg_ref[...], cos, sin_signed, gmat, first_half)
        q_ref[0, :, j * LANES:(j + 1) * LANES] = (qj * scale).astype(BF16)
    k_ref[0] = _norm_rope(seg(_C_AK), kg_ref[...], cos, sin_signed, gmat, first_half).astype(BF16)


def _inproj_call(x, mod_l, g, w_pack, cos2, sin2, qg2, kg2):
    b, l, d = x.shape
    tm = min(512, l)
    tk = min(ATT_TK, l)
    row = lambda bi, i: (bi, i, 0)
    full2 = lambda bi, i: (0, 0)
    rows_out = [
        (SDS((b, l, 256), F32), 256), (SDS((b, l, 256), F32), 256), (SDS((b, l, 256), F32), 256),
        (SDS((b, l, 128), F32), 128), (SDS((b, l, ATT_WIDTH), BF16), ATT_WIDTH),
        (SDS((b, l, ATT_KV), BF16), ATT_KV),
    ]
    out_specs = [pl.BlockSpec((1, tm, w), row) for _, w in rows_out]
    out_shape = [s for s, _ in rows_out]
    out_specs.append(pl.BlockSpec((1, tm // tk, ATT_KV, tk), lambda bi, i: (bi, i, 0, 0)))
    out_shape.append(SDS((b, l // tk, ATT_KV, tk), BF16))
    out_specs.append(pl.BlockSpec((1, tm, 768), row))
    out_shape.append(SDS((b, l, 768), F32))
    return pl.pallas_call(
        functools.partial(_inproj_kernel, tk=tk),
        grid=(b, l // tm),
        in_specs=[
            pl.BlockSpec((1, tm, d), row),
            pl.BlockSpec((1, 6, d), lambda bi, i: (bi, 0, 0)),
            pl.BlockSpec((1, d), full2),
            pl.BlockSpec((d, _NPACK), full2),
            pl.BlockSpec((tm, LANES), lambda bi, i: (i, 0)),
            pl.BlockSpec((tm, LANES), lambda bi, i: (i, 0)),
            pl.BlockSpec((1, LANES), full2),
            pl.BlockSpec((1, LANES), full2),
        ],
        out_specs=out_specs,
        out_shape=out_shape,
        compiler_params=_params("parallel", "parallel"),
        name="in_proj",
    )(x, mod_l, g, w_pack, cos2, sin2, qg2, kg2)


def _attn_kernel(q_ref, k_ref, vt_ref, o_ref, acc_ref, s_ref, p_ref, *, tq):
    nchunk = vt_ref.shape[1]
    tk = vt_ref.shape[3]
    lane = lax.broadcasted_iota(jnp.int32, (tq, LANES), 1)
    lower = lane < HEAD_DIM
    ones_rows = ATT_ONES_ROWS
    vrows = HEAD_DIM + ones_rows
    row = lax.broadcasted_iota(jnp.int32, (vrows, tk), 0)
    q = q_ref[0].astype(F32)
    streams = []
    qts = []
    for j in range(ATT_KV_HEADS):
        keep = lower if j == 0 else jnp.logical_not(lower)
        for hp in range(ATT_GROUP // 2):
            cb = j * (ATT_GROUP // 2) + hp
            qc = q[:, cb * LANES:(cb + 1) * LANES]
            qsw = pltpu.roll(qc, HEAD_DIM, 1)
            first, second = (qc, qsw) if j == 0 else (qsw, qc)
            cols = [jnp.where(keep, first, 0.0).T, jnp.where(keep, second, 0.0).T]
            qts.append(jnp.concatenate(cols, axis=1).astype(BF16))
            streams.append((j, cb))
    qt_all = jnp.concatenate(qts, axis=1)
    half = ATT_GROUP * tq
    acc_ref[...] = jnp.zeros(acc_ref.shape, F32)
    p_ref[1] = jnp.zeros(p_ref.shape[1:], BF16)
    ones = jnp.ones((vrows, tk), BF16)

    def scores(c):
        kc = k_ref[0, pl.ds(pl.multiple_of(c * tk, tk), tk), :]
        return jnp.dot(kc, qt_all, preferred_element_type=F32)

    def add_values(c, slot, alpha):
        vt = vt_ref[0, c]
        for j in range(ATT_KV_HEADS):
            r0 = j * (LANES - vrows)
            vm = jnp.where((row < HEAD_DIM) if j == 0 else (row >= ones_rows), vt[r0:r0 + vrows, :], ones)
            pj = p_ref[slot, :, j * half:(j + 1) * half]
            acc_ref[j] = acc_ref[j] * alpha[:, j * half:(j + 1) * half] + jnp.dot(vm, pj, preferred_element_type=F32)

    s_ref[0] = scores(0)

    def step(c, slot, carry):
        m_prev, alpha_prev = carry
        s_ref[1 - slot] = scores(jnp.minimum(c + 1, nchunk - 1))
        add_values(jnp.maximum(c - 1, 0), 1 - slot, alpha_prev)
        st = s_ref[slot]
        m_new = jnp.maximum(m_prev, jnp.max(st, axis=0, keepdims=True))
        p_ref[slot] = jnp.exp((st - m_new).astype(BF16))
        return m_new, jnp.exp(m_prev - m_new)

    def body(i, carry):
        return step(2 * i + 1, 1, step(2 * i, 0, carry))

    assert nchunk % 2 == 0
    init = (jnp.full((1, 2 * half), -jnp.inf, F32), jnp.ones((1, 2 * half), F32))
    _, alpha = lax.fori_loop(0, nchunk // 2, body, init)
    add_values(nchunk - 1, 1, alpha)

    for j, cb in streams:
        hp = cb % (ATT_GROUP // 2)
        acc = acc_ref[j][:, hp * 2 * tq:(hp + 1) * 2 * tq]
        d0 = j * ones_rows
        denom = acc[(1 - j) * HEAD_DIM:(1 - j) * HEAD_DIM + 1, :]
        data = acc[d0:d0 + HEAD_DIM, :] / denom
        blk = jnp.concatenate([data[:, :tq], data[:, tq:]], axis=0)
        o_ref[0, :, cb * LANES:(cb + 1) * LANES] = blk.T.astype(BF16)


def _attn_call(q, k, vt):
    b, l, _ = q.shape
    tq = min(1024, l)
    nchunk, tk = vt.shape[1], vt.shape[3]
    nq = ATT_HEADS * tq
    return pl.pallas_call(
        functools.partial(_attn_kernel, tq=tq),
        grid=(b, l // tq),
        in_specs=[
            pl.BlockSpec((1, tq, ATT_WIDTH), lambda bi, i: (bi, i, 0)),
            pl.BlockSpec((1, l, ATT_KV), lambda bi, i: (bi, 0, 0)),
            pl.BlockSpec((1, nchunk, ATT_KV, tk), lambda bi, i: (bi, 0, 0, 0)),
        ],
        out_specs=pl.BlockSpec((1, tq, ATT_WIDTH), lambda bi, i: (bi, i, 0)),
        out_shape=SDS((b, l, ATT_WIDTH), BF16),
        scratch_shapes=[pltpu.VMEM((ATT_KV_HEADS, HEAD_DIM + ATT_ONES_ROWS, nq // ATT_KV_HEADS), F32),
                        pltpu.VMEM((2, tk, nq), F32), pltpu.VMEM((2, tk, nq), BF16)],
        compiler_params=_params("parallel", "parallel"),
        name="gqa_attention",
    )(q, k, vt)


def _log_sigmoid(x):
    return jnp.minimum(x, 0.0) - jnp.log(1.0 + jnp.exp(-jnp.abs(x)))


def _gla_kernel(qk_ref, v_ref, gg_ref, lr_ref, wdec_ref, bdec_ref, ng_ref, o_ref,
                qs_ref, ks_ref, a_ref, od_ref, *, rt):
    seq = qk_ref.shape[1]
    cb = GLA_BLOCK
    sb = GLA_STATE_BLOCK
    rowi = lax.broadcasted_iota(jnp.int32, (rt, GLA_QK), 0)
    rowmod = rowi % cb
    hd = lax.broadcasted_iota(jnp.int32, (GLA_QK, GLA_WIDTH), 0) // GLA_DK
    hc = lax.broadcasted_iota(jnp.int32, (GLA_QK, GLA_WIDTH), 1) // GLA_DV
    expand = jnp.where(hd == hc, 1.0, 0.0).astype(BF16)
    qk_head = lax.broadcasted_iota(jnp.int32, (rt, GLA_QK), 1) // GLA_DK
    v_head = lax.broadcasted_iota(jnp.int32, (rt, GLA_WIDTH), 1) // GLA_DV
    pair_i = lax.broadcasted_iota(jnp.int32, (rt, rt), 0)
    pair_j = lax.broadcasted_iota(jnp.int32, (rt, rt), 1)
    scale = GLA_DK ** -0.5

    def prep(t, _):
        r0 = pl.multiple_of(t * rt, rt)
        rows = pl.ds(r0, rt)
        q = qk_ref[0, rows, 0:GLA_QK] * scale
        k = qk_ref[0, rows, GLA_QK:2 * GLA_QK]
        v = v_ref[0, rows, :]
        lr = lr_ref[0, rows, :]
        od = jnp.zeros((rt, GLA_WIDTH), F32)
        half_sums = []
        for direction in range(2):
            la = _log_sigmoid(jnp.dot(lr, wdec_ref[direction], precision=HI, preferred_element_type=F32)
                              + bdec_ref[direction]) * (1.0 / GLA_GATE_NORM)
            pre = la
            suf = la
            s = 1
            while s < cb:
                pre = pre + jnp.where(rowmod >= s, pltpu.roll(pre, s, 0), 0.0)
                suf = suf + jnp.where(rowmod < cb - s, pltpu.roll(suf, rt - s, 0), 0.0)
                s *= 2
            tot = pre + suf - la
            sums = {}
            size = cb
            while size < sb:
                sums[size] = (pre, suf, la)
                upper = (rowi % (2 * size)) >= size
                t_below = pltpu.roll(tot, size, 0)
                t_above = pltpu.roll(tot, rt - size, 0)
                pre = pre + jnp.where(upper, t_below, 0.0)
                suf = suf + jnp.where(upper, 0.0, t_above)
                tot = tot + jnp.where(upper, t_below, t_above)
                size *= 2
            half_sums.append(sums)
            cum = pre if direction == 0 else suf
            rem = (suf if direction == 0 else pre) - la
            qs_ref[direction, rows, :] = q * jnp.exp(cum)
            ks_ref[direction, rows, :] = k * jnp.exp(rem)
            a_ref[direction, rows, :] = jnp.exp(tot)
            for delta in range(cb):
                if delta == 0:
                    a_mat = q * k
                    vj = v
                else:
                    if direction == 0:
                        sh = delta
                        valid = rowmod >= delta
                    else:
                        sh = rt - delta
                        valid = rowmod < cb - delta
                    kj = pltpu.roll(k, sh, 0)
                    cj = pltpu.roll(cum, sh, 0)
                    vj = pltpu.roll(v, sh, 0)
                    a_mat = jnp.where(valid, q * kj * jnp.exp(cum - cj), 0.0)
                se = jnp.dot(a_mat.astype(BF16), expand, preferred_element_type=F32)
                od = od + se * vj
        size = cb
        while size < sb:
            upper = (rowi % (2 * size)) >= size
            pre_f, suf_f, la_f = half_sums[0][size]
            pre_b, suf_b, la_b = half_sums[1][size]
            qf = (q * jnp.exp(jnp.where(upper, pre_f, suf_b))).astype(BF16)
            kf = (k * jnp.exp(jnp.where(upper, pre_b - la_b, suf_f - la_f))).astype(BF16)
            same_block = (pair_i // (2 * size)) == (pair_j // (2 * size))
            cross = ((pair_i % (2 * size)) >= size) != ((pair_j % (2 * size)) >= size)
            keep = jnp.logical_and(same_block, cross)
            for h in range(GLA_HEADS):
                qh = jnp.where(qk_head == h, qf, jnp.zeros_like(qf))
                sc = lax.dot_general(qh, kf, (((1,), (1,)), ((), ())), preferred_element_type=F32)
                vh = jnp.where(v_head == h, v, 0.0).astype(BF16)
                od = od + jnp.dot(jnp.where(keep, sc, 0.0).astype(BF16), vh, preferred_element_type=F32)
            size *= 2
        od_ref[rows, :] = od
        return 0

    lax.fori_loop(0, seq // rt, prep, 0)

    sr = lax.broadcasted_iota(jnp.int32, (GLA_WIDTH, GLA_QK), 0) // GLA_DV
    scol = lax.broadcasted_iota(jnp.int32, (GLA_WIDTH, GLA_QK), 1) // GLA_DK
    same_head = sr == scol
    nblk = seq // sb
    unroll = min(GLA_STEP_UNROLL, nblk)
    span = unroll * sb

    def step(i, states):
        states = list(states)
        for direction in range(2):
            base = i * span if direction == 0 else seq - (i + 1) * span
            rows = pl.ds(pl.multiple_of(base, span), span)
            qb = qs_ref[direction, rows, :].astype(BF16)
            kb = ks_ref[direction, rows, :].astype(BF16)
            vb = v_ref[0, rows, :].astype(BF16)
            ab = a_ref[direction, rows, :]
            st = states[direction]
            outs = [None] * unroll
            for u in (range(unroll) if direction == 0 else range(unroll - 1, -1, -1)):
                sl = slice(u * sb, (u + 1) * sb)
                outs[u] = lax.dot_general(qb[sl], st.astype(BF16), (((1,), (1,)), ((), ())),
                                          preferred_element_type=F32)
                kv = lax.dot_general(vb[sl], kb[sl], (((0,), (0,)), ((), ())), preferred_element_type=F32)
                st = st * ab[u * sb:u * sb + 1, :] + jnp.where(same_head, kv, 0.0)
            od_ref[rows, :] += jnp.concatenate(outs, axis=0)
            states[direction] = st
        return tuple(states)

    zero_state = jnp.zeros((GLA_WIDTH, GLA_QK), F32)
    lax.fori_loop(0, nblk // unroll, step, (zero_state, zero_state))

    gr = lax.broadcasted_iota(jnp.int32, (GLA_WIDTH, GLA_WIDTH), 0) // GLA_DV
    gc = lax.broadcasted_iota(jnp.int32, (GLA_WIDTH, GLA_WIDTH), 1) // GLA_DV
    gmat = jnp.where(gr == gc, 1.0 / GLA_DV, 0.0).astype(BF16)

    def fin(t, _):
        rows = pl.ds(pl.multiple_of(t * rt, rt), rt)
        o = od_ref[rows, :]
        ms = jnp.dot((o * o).astype(BF16), gmat, preferred_element_type=F32)
        o = o * lax.rsqrt(ms + EPS) * ng_ref[...]
        o_ref[0, rows, :] = (o * _silu(gg_ref[0, rows, :])).astype(BF16)
        return 0

    lax.fori_loop(0, seq // rt, fin, 0)


def _gla_call(gqk, gv, gg, glr, wdec_pad, bdec, ng4):
    b, l, _ = gqk.shape
    rt = min(256, l)
    once = dict(pipeline_mode=pl.Buffered(1))
    bmap = lambda bi: (bi, 0, 0)
    return pl.pallas_call(
        functools.partial(_gla_kernel, rt=rt),
        grid=(b,),
        in_specs=[
            pl.BlockSpec((1, l, 2 * GLA_QK), bmap, **once),
            pl.BlockSpec((1, l, GLA_WIDTH), bmap, **once),
            pl.BlockSpec((1, l, GLA_WIDTH), bmap, **once),
            pl.BlockSpec((1, l, LANES), bmap, **once),
            pl.BlockSpec((2, LANES, GLA_QK), lambda bi: (0, 0, 0)),
            pl.BlockSpec((2, 1, GLA_QK), lambda bi: (0, 0, 0)),
            pl.BlockSpec((1, GLA_WIDTH), lambda bi: (0, 0)),
        ],
        out_specs=pl.BlockSpec((1, l, GLA_WIDTH), bmap),
        out_shape=SDS((b, l, GLA_WIDTH), BF16),
        scratch_shapes=[
            pltpu.VMEM((2, l, GLA_QK), F32),
            pltpu.VMEM((2, l, GLA_QK), F32),
            pltpu.VMEM((2, l, GLA_QK), F32),
            pltpu.VMEM((l, GLA_WIDTH), F32),
        ],
        compiler_params=_params("parallel"),
        name="gla",
    )(gqk, gv, gg, glr, wdec_pad, bdec, ng4)


def _hy_pre_kernel(x0_ref, x1_ref, v_ref, w0_ref, w1_ref, wv_ref, b0_ref, b1_ref, bv_ref, z_ref, x0o_ref, *, rt):
    seq = x0_ref.shape[1]
    rowid = lax.broadcasted_iota(jnp.int32, (rt, LANES), 0)

    def conv(ref, w_ref, b_ref, r0):
        cur = ref[0, pl.ds(r0, rt), :]
        prev = ref[0, pl.ds(pl.multiple_of(jnp.maximum(r0 - 8, 0), 8), 8), :][7:8, :]
        nxt = ref[0, pl.ds(pl.multiple_of(jnp.minimum(r0 + rt, seq - 8), 8), 8), :][0:1, :]
        prev = jnp.where(r0 > 0, prev, 0.0)
        nxt = jnp.where(r0 + rt < seq, nxt, 0.0)
        up = jnp.where(rowid == 0, prev, pltpu.roll(cur, 1, 0))
        down = jnp.where(rowid == rt - 1, nxt, pltpu.roll(cur, rt - 1, 0))
        return up * w_ref[0:1, :] + cur * w_ref[1:2, :] + down * w_ref[2:3, :] + b_ref[...]

    def body(t, _):
        r0 = pl.multiple_of(t * rt, rt)
        rows = pl.ds(r0, rt)
        x0o_ref[0, rows, :] = conv(x0_ref, w0_ref, b0_ref, r0)
        z_ref[0, rows, :] = conv(v_ref, wv_ref, bv_ref, r0) * conv(x1_ref, w1_ref, b1_ref, r0)
        return 0

    lax.fori_loop(0, seq // rt, body, 0)


def _hy_pre_call(hy, conv_w, conv_b):
    b, l, w3 = hy.shape
    hw = w3 // 3
    nj = hw // LANES
    rt = min(512, l)
    xs = lambda off: pl.BlockSpec((1, l, LANES), lambda bi, j: (bi, 0, off * nj + j))
    ws = lambda off: pl.BlockSpec((3, LANES), lambda bi, j: (0, off * nj + j))
    bs = lambda off: pl.BlockSpec((1, LANES), lambda bi, j: (0, off * nj + j))
    out = pl.BlockSpec((1, l, LANES), lambda bi, j: (bi, 0, j))
    return pl.pallas_call(
        functools.partial(_hy_pre_kernel, rt=rt),
        grid=(b, nj),
        in_specs=[xs(0), xs(1), xs(2), ws(0), ws(1), ws(2), bs(0), bs(1), bs(2)],
        out_specs=[out, out],
        out_shape=[SDS((b, l, hw), F32), SDS((b, l, hw), F32)],
        compiler_params=_params("parallel", "parallel"),
        name="hyena_short_conv",
    )(hy, hy, hy, conv_w, conv_w, conv_w, conv_b, conv_b, conv_b)


def _hy_filter_kernel(z_ref, w1_ref, b1_ref, w2_ref, b2_ref, w3_ref, fr_ref, dl_ref, o_ref, *, seq, tr):
    z = z_ref[...]
    fr = fr_ref[...]
    h = jnp.sin(fr * (jnp.dot(z, w1_ref[...], precision=HI, preferred_element_type=F32) + b1_ref[...]))
    for i in range(w2_ref.shape[0]):
        h = jnp.sin(fr * (jnp.dot(h, w2_ref[i], precision=HI, preferred_element_type=F32) + b2_ref[i]))
    hw = jnp.dot(h, w3_ref[...], precision=HI, preferred_element_type=F32)
    c = hw.shape[1] // 2
    window = jnp.exp(-z[:, 0:1] * dl_ref[...])
    m = pl.program_id(0) * tr + lax.broadcasted_iota(jnp.int32, (tr, c), 0)
    tap = jnp.where(m >= seq, hw[:, :c], hw[:, c:]) * window
    o_ref[...] = jnp.where(m == 0, 0.0, tap)


def _hy_filter_call(feat, w1p, b1, w2, b2, w3, freq, deltas, seq):
    n2, _ = feat.shape
    od = w1p.shape[1]
    c2 = w3.shape[1]
    tr = min(1024, n2)
    full = lambda *shape: pl.BlockSpec(shape, lambda i: (0,) * len(shape))
    return pl.pallas_call(
        functools.partial(_hy_filter_kernel, seq=seq, tr=tr),
        grid=(n2 // tr,),
        in_specs=[
            pl.BlockSpec((tr, LANES), lambda i: (i, 0)),
            full(LANES, od), full(1, od), full(w2.shape[0], od, od), full(w2.shape[0], 1, od),
            full(od, c2), full(1, od), full(1, c2 // 2),
        ],
        out_specs=pl.BlockSpec((tr, c2 // 2), lambda i: (i, 0)),
        out_shape=SDS((n2, c2 // 2), F32),
        compiler_params=_params("parallel"),
        name="hyena_filter",
    )(feat, w1p, b1, w2, b2, w3, freq, deltas)


def _hy_conv_kernel(z_ref, f_ref, y_ref, acc_ref, *, nb, bp):
    cg = z_ref.shape[0]
    p = HY_BLOCK
    ii = lax.broadcasted_iota(jnp.int32, (p, p), 0)
    jj = lax.broadcasted_iota(jnp.int32, (p, p), 1)
    upper = jj >= ii

    rows = nb * bp
    sub = 8
    shifts = sorted({(bp * d) % sub for d in range(-(nb - 1), nb)})

    def chan(ci, _):
        pad = jnp.zeros((sub, p), F32)
        zfull = jnp.concatenate([pad, z_ref[ci], pad], axis=0)
        zs = {m: (zfull if m == 0 else pltpu.roll(zfull, m, 0)) for m in shifts}
        acc_ref[...] = jnp.zeros(acc_ref.shape, F32)

        def skew(k):
            taps = jnp.broadcast_to(f_ref[ci, k:k + 1, :], (p, p))
            return pltpu.roll(taps, 0, 1, stride=1, stride_axis=0).astype(BF16)

        hi = skew(0)
        for k in range(2 * nb - 1):
            delta = k - (nb - 1)
            lo, hi = hi, skew(k + 1)
            w = jnp.where(upper, hi, lo)
            s = bp * delta
            o0 = max(0, s) // sub * sub
            o1 = -(-min(rows, rows + s) // sub) * sub
            m = s % sub
            src = o0 - s + sub + m
            lhs = zs[m][src:src + (o1 - o0), :].astype(BF16)
            acc_ref[o0:o1, :] += jnp.dot(lhs, w, preferred_element_type=F32)
        y_ref[ci] = acc_ref[...]
        return 0

    lax.fori_loop(0, cg, chan, 0)


def _hy_conv_call(zt, ft, nb, bp):
    c, rows, p = zt.shape
    cg = 8
    return pl.pallas_call(
        functools.partial(_hy_conv_kernel, nb=nb, bp=bp),
        grid=(c // cg,),
        in_specs=[
            pl.BlockSpec((cg, rows, p), lambda g: (g, 0, 0)),
            pl.BlockSpec((cg, 2 * nb, p), lambda g: (g, 0, 0)),
        ],
        out_specs=pl.BlockSpec((cg, rows, p), lambda g: (g, 0, 0)),
        out_shape=SDS((c, rows, p), F32),
        scratch_shapes=[pltpu.VMEM((rows, p), F32)],
        compiler_params=_params("parallel"),
        name="hyena_long_conv",
    )(zt, ft)


def _outproj_kernel(x_ref, ogla_ref, oatt_ref, yc_ref, z_ref, x0_ref, skip_ref, w_ref, mod_ref,
                    gpost_ref, gpre_ref, rw_ref, xo_ref, h2_ref, aff_ref, *, n_exp):
    ohy = (x0_ref[0] * (yc_ref[0] + z_ref[0] * skip_ref[...])).astype(BF16)
    g0 = GLA_WIDTH
    g1 = GLA_WIDTH + ATT_WIDTH
    y = jnp.dot(ogla_ref[0], w_ref[0:g0, :], preferred_element_type=F32)
    y = y + jnp.dot(oatt_ref[0], w_ref[g0:g1, :], preferred_element_type=F32)
    y = y + jnp.dot(ohy, w_ref[g1:, :], preferred_element_type=F32)
    g_m = mod_ref[0, 2:3, :]
    sh_f = mod_ref[0, 3:4, :]
    sc_f = mod_ref[0, 4:5, :]
    x1 = x_ref[0] + g_m * (_rms(y) * gpost_ref[...])
    xo_ref[0] = x1
    h2 = _rms(x1) * gpre_ref[...] * (1.0 + sc_f) + sh_f
    h2_ref[0] = h2.T.astype(BF16)
    h_hi = h2.astype(BF16)
    h_lo = (h2 - h_hi.astype(F32)).astype(BF16)
    logits = (jnp.dot(h_hi, rw_ref[0], preferred_element_type=F32)
              + jnp.dot(h_lo, rw_ref[0], preferred_element_type=F32)
              + jnp.dot(h_hi, rw_ref[1], preferred_element_type=F32))
    lane = lax.broadcasted_iota(jnp.int32, logits.shape, 1)
    logits = jnp.where(lane < n_exp, logits, -jnp.inf)
    e = jnp.exp(logits - jnp.max(logits, axis=-1, keepdims=True))
    aff_ref[0] = e / jnp.sum(e, axis=-1, keepdims=True)


def _outproj_call(x, ogla, oatt, yc, z, x0, skip, w_out, mod_l, gpost, gpre, rw_pad, n_exp):
    b, l, d = x.shape
    tm = min(512, l)
    row = lambda bi, i: (bi, i, 0)
    full2 = lambda bi, i: (0, 0)
    hw = yc.shape[-1]
    return pl.pallas_call(
        functools.partial(_outproj_kernel, n_exp=n_exp),
        grid=(b, l // tm),
        in_specs=[
            pl.BlockSpec((1, tm, d), row),
            pl.BlockSpec((1, tm, GLA_WIDTH), row),
            pl.BlockSpec((1, tm, ATT_WIDTH), row),
            pl.BlockSpec((1, tm, hw), row),
            pl.BlockSpec((1, tm, hw), row),
            pl.BlockSpec((1, tm, hw), row),
            pl.BlockSpec((1, hw), full2),
            pl.BlockSpec((d, d), full2),
            pl.BlockSpec((1, 6, d), lambda bi, i: (bi, 0, 0)),
            pl.BlockSpec((1, d), full2),
            pl.BlockSpec((1, d), full2),
            pl.BlockSpec((2, d, LANES), lambda bi, i: (0, 0, 0)),
        ],
        out_specs=[pl.BlockSpec((1, tm, d), row), pl.BlockSpec((1, d, tm), lambda bi, i: (bi, 0, i)),
                   pl.BlockSpec((1, tm, LANES), row)],
        out_shape=[SDS((b, l, d), F32), SDS((b, d, l), BF16), SDS((b, l, LANES), F32)],
        compiler_params=_params("parallel", "parallel"),
        name="out_proj_router",
    )(x, ogla, oatt, yc, z, x0, skip, w_out, mod_l, gpost, gpre, rw_pad)


def _route_kernel(aff_ref, pos_ref, cum_ref, *, cap):
    aff = aff_ref[0]
    n_exp, t = aff.shape

    def bisect(_, bounds):
        lo, hi = bounds
        mid = 0.5 * (lo + hi)
        ok = jnp.sum(jnp.where(aff >= mid, 1.0, 0.0), axis=1, keepdims=True) >= cap
        return jnp.where(ok, mid, lo), jnp.where(ok, hi, mid)

    lo, hi = lax.fori_loop(0, ROUTE_BISECTIONS, bisect,
                           (jnp.zeros((n_exp, 1), F32), jnp.full((n_exp, 1), 2.0, F32)))
    need = cap - jnp.sum(jnp.where(aff >= hi, 1.0, 0.0), axis=1, keepdims=True)
    r = lax.broadcasted_iota(jnp.int32, (LANES, LANES), 0)
    c = lax.broadcasted_iota(jnp.int32, (LANES, LANES), 1)
    tri = jnp.where(r <= c, 1.0, 0.0).astype(BF16)
    eq_run = jnp.zeros((n_exp, 1), F32)
    sel_run = jnp.zeros((n_exp, 1), F32)
    runs = []
    for j in range(t // LANES):
        sl = slice(j * LANES, (j + 1) * LANES)
        aj = aff[:, sl]
        eq_j = jnp.where(aj >= hi, 0.0, jnp.where(aj >= lo, 1.0, 0.0))
        eq_cum = jnp.dot(eq_j.astype(BF16), tri, preferred_element_type=F32) + eq_run
        sel_j = jnp.where(aj >= hi, 1.0, jnp.where(eq_cum <= need, eq_j, 0.0))
        sel_cum = jnp.dot(sel_j.astype(BF16), tri, preferred_element_type=F32) + sel_run
        pos_ref[0, :, sl] = jnp.where(sel_j > 0.0, sel_cum - 1.0, -1.0).astype(jnp.int32)
        eq_run = eq_cum[:, LANES - 1:LANES]
        sel_run = sel_cum[:, LANES - 1:LANES]
        runs.append(sel_run)
    cum_ref[0] = jnp.concatenate(runs, axis=1).astype(jnp.int32)


def _route_call(aff_t, cap):
    b, n_exp, t = aff_t.shape
    rows = b * n_exp
    pos, cum = pl.pallas_call(
        functools.partial(_route_kernel, cap=cap),
        grid=(1,),
        in_specs=[pl.BlockSpec((1, rows, t), lambda i: (0, 0, 0))],
        out_specs=[pl.BlockSpec((1, rows, t), lambda i: (0, 0, 0)),
                   pl.BlockSpec((1, rows, t // LANES), lambda i: (0, 0, 0))],
        out_shape=[SDS((1, rows, t), jnp.int32), SDS((1, rows, t // LANES), jnp.int32)],
        compiler_params=_params("arbitrary"),
        name="ec_route",
    )(aff_t.reshape(1, rows, t))
    return pos.reshape(rows, 1, t), cum.reshape(rows, t // LANES)


def _chunk_ranges(cum, cap, tc):
    per = tc // LANES
    cum_c = cum[:, per - 1::per]
    first = jnp.arange(0, cap, EC_SLOT_BLOCK, dtype=jnp.int32)
    chunk_of = lambda s: jnp.sum(cum_c[:, None, :] <= s[None, :, None], axis=-1).astype(jnp.int32)
    return chunk_of(first).reshape(-1), chunk_of(first + EC_SLOT_BLOCK - 1).reshape(-1)


def _one_hot(pos_ref, off, tc, first_slot):
    slot = first_slot + lax.broadcasted_iota(jnp.int32, (EC_SLOT_BLOCK, tc), 0)
    return jnp.where(slot == pos_ref[0, :, pl.ds(off, tc)], 1.0, 0.0).astype(BF16)


def _dispatch_kernel(lo_ref, hi_ref, pos_ref, aff_ref, ht_ref, xgt_ref, gc_ref, acc_ref, gacc_ref, *, tc, no_slot):
    nr = pl.num_programs(2)
    r = pl.program_id(2)
    idx = (pl.program_id(0) * pl.num_programs(1) + pl.program_id(1)) * nr + r
    acc_ref[...] = jnp.zeros(acc_ref.shape, F32)
    gacc_ref[...] = jnp.zeros(gacc_ref.shape, F32)
    nt = (((1,), (1,)), ((), ()))

    lo = lo_ref[idx]
    hi = hi_ref[idx]

    def body(i, _):
        c0 = lo + 2 * i
        chunks = ((c0, r * EC_SLOT_BLOCK),
                  (jnp.minimum(c0 + 1, hi), jnp.where(c0 + 1 <= hi, r * EC_SLOT_BLOCK, no_slot)))
        part = gpart = None
        for c, first_slot in chunks:
            off = pl.multiple_of(c * tc, tc)
            sel = _one_hot(pos_ref, off, tc, first_slot)
            d = lax.dot_general(ht_ref[0, :, pl.ds(off, tc)], sel, nt, preferred_element_type=F32)
            g = aff_ref[0, :, pl.ds(off, tc)]
            g1 = g.astype(BF16).astype(F32)
            g2 = (g - g1).astype(BF16).astype(F32)
            g3 = g - g1 - g2
            prow = lax.broadcasted_iota(jnp.int32, (gacc_ref.shape[0], tc), 0)
            pieces = jnp.where(prow == 0, g1, jnp.where(prow == 1, g2, jnp.where(prow == 2, g3, 0.0))).astype(BF16)
            gd = lax.dot_general(pieces, sel, nt, preferred_element_type=F32)
            part = d if part is None else part + d
            gpart = gd if gpart is None else gpart + gd
        acc_ref[...] += part
        gacc_ref[...] += gpart
        return 0

    lax.fori_loop(0, (hi - lo + 2) // 2, body, 0)
    xgt_ref[0] = acc_ref[...].astype(BF16)
    gc_ref[0] = jnp.sum(gacc_ref[...], axis=0, keepdims=True)


def _dispatch_call(lo, hi, pos_rows, aff_rows, h2t, n_exp, cap):
    b, d, t = h2t.shape
    tc = min(EC_TOKEN_CHUNK, t)
    nr = cap // EC_SLOT_BLOCK
    row = lambda bi, e, r, lo_r, hi_r: (bi * n_exp + e, 0, 0)
    out = lambda bi, e, r, lo_r, hi_r: (e, 0, bi * nr + r)
    return pl.pallas_call(
        functools.partial(_dispatch_kernel, tc=tc, no_slot=cap),
        grid_spec=pltpu.PrefetchScalarGridSpec(
            num_scalar_prefetch=2,
            grid=(b, n_exp, nr),
            in_specs=[
                pl.BlockSpec((1, 1, t), row),
                pl.BlockSpec((1, 1, t), row),
                pl.BlockSpec((1, d, t), lambda bi, e, r, lo_r, hi_r: (bi, 0, 0)),
            ],
            out_specs=[pl.BlockSpec((1, d, EC_SLOT_BLOCK), out), pl.BlockSpec((1, 1, EC_SLOT_BLOCK), out)],
            scratch_shapes=[pltpu.VMEM((d, EC_SLOT_BLOCK), F32), pltpu.VMEM((16, EC_SLOT_BLOCK), F32)],
        ),
        out_shape=[SDS((n_exp, d, b * cap), BF16), SDS((n_exp, 1, b * cap), F32)],
        compiler_params=_params("parallel", "arbitrary", "arbitrary"),
        name="ec_dispatch",
    )(lo, hi, pos_rows, aff_rows, h2t)


def _ffn_kernel(xgt_ref, gc_ref, wg_ref, wu_ref, wd_ref, yt_ref, xg_ref, acc_ref, *, mt):
    f = pl.program_id(1)
    m = xg_ref.shape[0]

    @pl.when(f == 0)
    def _():
        acc_ref[...] = jnp.zeros(acc_ref.shape, F32)
        for i in range(m // mt):
            xg_ref[i * mt:(i + 1) * mt, :] = xgt_ref[0, :, i * mt:(i + 1) * mt].T

    wg = wg_ref[0, 0].astype(BF16)
    wu = wu_ref[0, 0].astype(BF16)
    wd = wd_ref[0, 0].astype(BF16)
    for i in range(m // mt):
        rows = slice(i * mt, (i + 1) * mt)
        xb = xg_ref[rows, :]
        a = jnp.dot(xb, wg, preferred_element_type=F32)
        u = jnp.dot(xb, wu, preferred_element_type=F32)
        acc_ref[rows, :] += jnp.dot((_silu(a) * u).astype(BF16), wd, preferred_element_type=F32)

    @pl.when(f == pl.num_programs(1) - 1)
    def _():
        for i in range(m // mt):
            cols = slice(i * mt, (i + 1) * mt)
            yt_ref[0, :, cols] = (acc_ref[cols, :].T * gc_ref[0, :, cols]).astype(BF16)


def _ffn_call(xgt, gc, w_gate, w_up, w_down, layer):
    n_exp, d, m = xgt.shape
    ff = w_gate.shape[3]
    tf = 256
    mt = min(512, m)
    return pl.pallas_call(
        functools.partial(_ffn_kernel, mt=mt),
        grid=(n_exp, ff // tf),
        in_specs=[
            pl.BlockSpec((1, d, m), lambda e, f: (e, 0, 0)),
            pl.BlockSpec((1, 1, m), lambda e, f: (e, 0, 0)),
            pl.BlockSpec((1, 1, d, tf), lambda e, f: (layer, e, 0, f)),
            pl.BlockSpec((1, 1, d, tf), lambda e, f: (layer, e, 0, f)),
            pl.BlockSpec((1, 1, tf, d), lambda e, f: (layer, e, f, 0)),
        ],
        out_specs=pl.BlockSpec((1, d, m), lambda e, f: (e, 0, 0)),
        out_shape=SDS((n_exp, d, m), BF16),
        scratch_shapes=[pltpu.VMEM((m, d), BF16), pltpu.VMEM((m, d), F32)],
        compiler_params=_params("parallel", "arbitrary"),
        name="expert_ffn",
    )(xgt, gc, w_gate, w_up, w_down)


def _combine_kernel(lo_ref, hi_ref, pos_ref, yt_ref, o_ref, *, tc, no_slot):
    ne = pl.num_programs(1)
    nr = pl.num_programs(2)
    e = pl.program_id(1)
    r = pl.program_id(2)
    idx = (pl.program_id(0) * ne + e) * nr + r

    @pl.when(jnp.logical_and(e == 0, r == 0))
    def _():
        o_ref[...] = jnp.zeros(o_ref.shape, F32)

    yb = yt_ref[0]
    lo = lo_ref[idx]
    hi = hi_ref[idx]

    def body(i, _):
        c0 = lo + 2 * i
        chunks = ((c0, r * EC_SLOT_BLOCK),
                  (jnp.minimum(c0 + 1, hi), jnp.where(c0 + 1 <= hi, r * EC_SLOT_BLOCK, no_slot)))
        adds = []
        for c, first_slot in chunks:
            off = pl.multiple_of(c * tc, tc)
            adds.append((off, jnp.dot(yb, _one_hot(pos_ref, off, tc, first_slot), preferred_element_type=F32)))
        for off, contrib in adds:
            o_ref[0, :, pl.ds(off, tc)] += contrib
        return 0

    lax.fori_loop(0, (hi - lo + 2) // 2, body, 0)


def _combine_call(lo, hi, pos_rows, yt, b, t, cap):
    n_exp, d, _ = yt.shape
    tc = min(EC_TOKEN_CHUNK, t)
    nr = cap // EC_SLOT_BLOCK
    return pl.pallas_call(
        functools.partial(_combine_kernel, tc=tc, no_slot=cap),
        grid_spec=pltpu.PrefetchScalarGridSpec(
            num_scalar_prefetch=2,
            grid=(b, n_exp, nr),
            in_specs=[
                pl.BlockSpec((1, 1, t), lambda bi, e, r, lo_r, hi_r: (bi * n_exp + e, 0, 0)),
                pl.BlockSpec((1, d, EC_SLOT_BLOCK), lambda bi, e, r, lo_r, hi_r: (e, 0, bi * nr + r)),
            ],
            out_specs=pl.BlockSpec((1, d, t), lambda bi, e, r, lo_r, hi_r: (bi, 0, 0)),
        ),
        out_shape=SDS((b, d, t), F32),
        compiler_params=_params("parallel", "arbitrary", "arbitrary"),
        name="ec_combine",
    )(lo, hi, pos_rows, yt)


def _ffn_residual_kernel(acc_ref, x_ref, mod_ref, g_ref, o_ref):
    y = acc_ref[0].T
    g_f = mod_ref[0, 5:6, :]
    o_ref[0] = x_ref[0] + g_f * (_rms(y) * g_ref[...])


def _ffn_residual_call(acc_t, x, mod_l, gpost):
    b, t, d = x.shape
    tt = min(512, t)
    return pl.pallas_call(
        _ffn_residual_kernel,
        grid=(b, t // tt),
        in_specs=[
            pl.BlockSpec((1, d, tt), lambda bi, i: (bi, 0, i)),
            pl.BlockSpec((1, tt, d), lambda bi, i: (bi, i, 0)),
            pl.BlockSpec((1, 6, d), lambda bi, i: (bi, 0, 0)),
            pl.BlockSpec((1, d), lambda bi, i: (0, 0)),
        ],
        out_specs=pl.BlockSpec((1, tt, d), lambda bi, i: (bi, i, 0)),
        out_shape=SDS((b, t, d), F32),
        compiler_params=_params("parallel", "parallel"),
        name="ffn_residual",
    )(acc_t, x, mod_l, gpost)


def _rope_tables(seq):
    n_rows = seq // GRID_W
    row = jnp.repeat(jnp.arange(n_rows), GRID_W).astype(F32)
    col = jnp.tile(jnp.arange(GRID_W), n_rows).astype(F32)
    inv = 1.0 / (ROPE_THETA ** (jnp.arange(0, ROPE_HALF, 2, dtype=F32) / ROPE_HALF))
    ar = row[:, None] * inv[None]
    ac = col[:, None] * inv[None]
    ang = jnp.concatenate([ar, ar, ac, ac], axis=-1)
    cos, sin = jnp.cos(ang), jnp.sin(ang)
    first_half = (jnp.arange(HEAD_DIM) % ROPE_HALF) < (ROPE_HALF // 2)
    sin_signed = jnp.where(first_half[None], -sin, sin)
    return jnp.tile(cos, (1, 2)), jnp.tile(sin_signed, (1, 2))


def _filter_features(seq):
    pos = jnp.abs(jnp.arange(2 * seq) - seq).astype(F32)
    t = pos / (seq - 1)
    f = jnp.linspace(1e-4, HY_BANDS - 1, HY_BANDS, dtype=F32)
    ang = (2.0 * math.pi * pos / seq)[:, None] * f[None]
    z = jnp.concatenate([t[:, None], jnp.cos(ang), -jnp.sin(ang)], axis=-1)
    return jnp.pad(z, ((0, 0), (0, LANES - HY_EMB)))


def kernel(x, c, ada_w, ada_b, mix_pre_g, mix_post_g, w_in, gla_w_dec, gla_b_dec, gla_norm_g, q_norm_g, k_norm_g, hy_conv_w, hy_conv_b, hy_w1, hy_b1, hy_w2, hy_b2, hy_w3, hy_freq, hy_skip, w_out, ffn_pre_g, ffn_post_g, router_w, exp_w_gate, exp_w_up, exp_w_down):
    b, seq, d = x.shape
    depth = ada_w.shape[0]
    n_exp = router_w.shape[-1]
    cap = EC_FACTOR * seq // n_exp
    hw = hy_skip.shape[-1]
    nb = seq // HY_BLOCK
    assert seq % HY_BLOCK == 0 and (nb * b) % 8 == 0

    mod = _ada_call(c, ada_w, ada_b).reshape(depth, b, 6, d)
    cos2, sin2 = _rope_tables(seq)
    feat = _filter_features(seq)
    deltas = jnp.abs(jnp.linspace(HY_MIN_DECAY, HY_MAX_DECAY, hw, dtype=F32))[None]

    for l in range(depth):
        w = w_in[l]
        w_pack = jnp.concatenate(
            [w[:, :768], jnp.pad(w[:, 768:800], ((0, 0), (0, LANES - 2 * GLA_LOWRANK))), w[:, 800:]], axis=1
        ).astype(BF16)
        gqk, gv, gg, glr, aq, ak, av, hy = _inproj_call(
            x, mod[l], mix_pre_g[l][None], w_pack, cos2, sin2,
            jnp.tile(q_norm_g[l], 2)[None], jnp.tile(k_norm_g[l], 2)[None])
        o_att = _attn_call(aq, ak, av)

        wdec_pad = jnp.zeros((2, LANES, GLA_QK), F32)
        wdec_pad = wdec_pad.at[0, :GLA_LOWRANK].set(gla_w_dec[l, 0]).at[1, GLA_LOWRANK:2 * GLA_LOWRANK].set(gla_w_dec[l, 1])
        o_gla = _gla_call(gqk, gv, gg, glr, wdec_pad, gla_b_dec[l][:, None, :],
                          jnp.tile(gla_norm_g[l], GLA_HEADS)[None])

        z, x0 = _hy_pre_call(hy, hy_conv_w[l], hy_conv_b[l][None])
        w1p = jnp.pad(hy_w1[l], ((0, LANES - HY_EMB), (0, 0)))
        filt = _hy_filter_call(feat, w1p, hy_b1[l][None], hy_w2[l], hy_b2[l][:, None, :], hy_w3[l],
                               hy_freq[l][None], deltas, seq)
        ft = filt.T.reshape(hw, 2 * nb, HY_BLOCK)
        zt = jnp.transpose(z.reshape(b, nb, HY_BLOCK, hw), (3, 1, 0, 2)).reshape(hw, nb * b, HY_BLOCK)
        yt = _hy_conv_call(zt, ft, nb, b).reshape(hw, nb, b, HY_BLOCK)
        yc = jnp.transpose(yt, (2, 1, 3, 0)).reshape(b, seq, hw)

        rw_full = jnp.pad(router_w[l], ((0, 0), (0, LANES - n_exp)))
        rw_hi = rw_full.astype(BF16)
        rw_pad = jnp.stack([rw_hi, (rw_full - rw_hi.astype(F32)).astype(BF16)])
        x, h2, aff = _outproj_call(x, o_gla, o_att, yc, z, x0, hy_skip[l][None], w_out[l].astype(BF16), mod[l],
                                   mix_post_g[l][None], ffn_pre_g[l][None], rw_pad, n_exp)

        aff_t = jnp.swapaxes(aff[..., :n_exp], 1, 2)
        pos_rows, cum = _route_call(aff_t, cap)
        lo, hi = _chunk_ranges(cum, cap, min(EC_TOKEN_CHUNK, seq))
        aff_rows = aff_t.reshape(b * n_exp, 1, seq)
        xgt, gc = _dispatch_call(lo, hi, pos_rows, aff_rows, h2, n_exp, cap)
        yt = _ffn_call(xgt, gc, exp_w_gate, exp_w_up, exp_w_down, l)
        acc_t = _combine_call(lo, hi, pos_rows, yt, b, seq, cap)
        x = _ffn_residual_call(acc_t, x, mod[l], ffn_post_g[l][None])
    return x
```

```python
import functools
import math

import jax
import jax.numpy as jnp
import numpy as np
from jax import lax
from jax.experimental import pallas as pl
from jax.experimental.pallas import tpu as pltpu

F32 = jnp.float32
BF16 = jnp.bfloat16
HI = lax.Precision.HIGHEST
SDS = jax.ShapeDtypeStruct

EPS = 1e-6
GRID_W = 64
HEAD_DIM = 64
GLA_HEADS = 4
GLA_DK = 32
GLA_DV = 64
GLA_QK = GLA_HEADS * GLA_DK
GLA_WIDTH = GLA_HEADS * GLA_DV
GLA_LOWRANK = 16
GLA_GATE_NORM = 16.0
GLA_BLOCK = 8
GLA_STATE_BLOCK = 64
GLA_STEP_UNROLL = 4
ATT_HEADS = 8
ATT_KV_HEADS = 2
ATT_GROUP = ATT_HEADS // ATT_KV_HEADS
ATT_WIDTH = ATT_HEADS * HEAD_DIM
ATT_KV = ATT_KV_HEADS * HEAD_DIM
ATT_TK = 256
ATT_ONES_ROWS = 16
ROPE_THETA = 10000.0
ROPE_HALF = HEAD_DIM // 2
HY_EMB = 33
HY_BANDS = (HY_EMB - 1) // 2
HY_TARGET = 1e-2
HY_FAST = 0.3
HY_SLOW = 1.5
HY_MIN_DECAY = math.log(HY_TARGET) / HY_SLOW
HY_MAX_DECAY = math.log(HY_TARGET) / HY_FAST
HY_BLOCK = 128
N_EXPERTS = 16
EC_FACTOR = 2
ROUTE_BISECTIONS = 160
EC_SLOT_BLOCK = 256
EC_TOKEN_CHUNK = 512
LANES = 128
VMEM_LIMIT = 56 * 1024 * 1024


def _params(*sem):
    return pltpu.CompilerParams(dimension_semantics=sem, vmem_limit_bytes=VMEM_LIMIT)


def _silu(x):
    return x * (1.0 / (1.0 + jnp.exp(-x)))


def _rms(x):
    return x * lax.rsqrt(jnp.mean(x * x, axis=-1, keepdims=True) + EPS)


def _ada_kernel(c_ref, w_ref, b_ref, o_ref):
    cond = _silu(c_ref[...])
    o_ref[0] = jnp.dot(cond, w_ref[0], precision=HI, preferred_element_type=F32) + b_ref[0]


def _ada_call(c, ada_w, ada_b):
    depth, d, n6 = ada_w.shape
    b = c.shape[0]
    tn = 1024
    return pl.pallas_call(
        _ada_kernel,
        grid=(depth, n6 // tn),
        in_specs=[
            pl.BlockSpec((b, d), lambda l, j: (0, 0)),
            pl.BlockSpec((1, d, tn), lambda l, j: (l, 0, j)),
            pl.BlockSpec((1, 1, tn), lambda l, j: (l, 0, j)),
        ],
        out_specs=pl.BlockSpec((1, b, tn), lambda l, j: (l, 0, j)),
        out_shape=SDS((depth, b, n6), F32),
        compiler_params=_params("arbitrary", "arbitrary"),
        name="ada_mod",
    )(c, ada_w, ada_b.reshape(depth, 1, n6))


_C_GQK = (0, 256)
_C_GV = (256, 512)
_C_GG = (512, 768)
_C_GLR = (768, 896)
_C_AQ = (896, 1408)
_C_AK = (1408, 1536)
_C_AV = (1536, 1664)
_C_HY = (1664, 2432)
_NPACK = 2432


def _head_mean_matrix():
    r = lax.broadcasted_iota(jnp.int32, (LANES, LANES), 0) // HEAD_DIM
    c = lax.broadcasted_iota(jnp.int32, (LANES, LANES), 1) // HEAD_DIM
    return jnp.where(r == c, 1.0 / HEAD_DIM, 0.0).astype(BF16)


def _norm_rope(a, gain, cos, sin_signed, gmat, first_half):
    ms = jnp.dot((a * a).astype(BF16), gmat, preferred_element_type=F32)
    xn = a * lax.rsqrt(ms + EPS) * gain
    rot = jnp.where(first_half, pltpu.roll(xn, LANES - ROPE_HALF // 2, 1), pltpu.roll(xn, ROPE_HALF // 2, 1))
    return xn * cos + rot * sin_signed


def _inproj_kernel(x_ref, mod_ref, g_ref, w_ref, cos_ref, sin_ref, qg_ref, kg_ref,
                   gqk_ref, gv_ref, gg_ref, glr_ref, q_ref, k_ref, vt_ref, hy_ref, *, tk):
    x = x_ref[0]
    sh = mod_ref[0, 0:1, :]
    sc = mod_ref[0, 1:2, :]
    h = (_rms(x) * g_ref[...] * (1.0 + sc) + sh).astype(BF16)

    def seg(c):
        return jnp.dot(h, w_ref[:, c[0]:c[1]], preferred_element_type=F32)

    gqk_ref[0] = seg(_C_GQK)
    gv_ref[0] = seg(_C_GV)
    gg_ref[0] = seg(_C_GG)
    glr_ref[0] = seg(_C_GLR)
    hy_ref[0] = seg(_C_HY)
    av = seg(_C_AV)
    for ci in range(av.shape[0] // tk):
        vt_ref[0, ci] = av[ci * tk:(ci + 1) * tk, :].T.astype(BF16)

    gmat = _head_mean_matrix()
    cos = cos_ref[...]
    sin_signed = sin_ref[...]
    lane = lax.broadcasted_iota(jnp.int32, cos.shape, 1)
    first_half = (lane % ROPE_HALF) < (ROPE_HALF // 2)
    aq = seg(_C_AQ)
    scale = HEAD_DIM ** -0.5
    for j in range(ATT_WIDTH // LANES):
        qj = _norm_rope(aq[:, j * LANES:(j + 1) * LANES], qg_ref[...], cos, sin_signed, gmat, first_half)
        q_ref[0, :, j * LANES:(j + 1) * LANES] = (qj * scale).astype(BF16)
    k_ref[0] = _norm_rope(seg(_C_AK), kg_ref[...], cos, sin_signed, gmat, first_half).astype(BF16)


def _inproj_call(x, mod_l, g, w_pack, cos2, sin2, qg2, kg2):
    b, l, d = x.shape
    tm = min(1024, l)
    tk = min(ATT_TK, l)
    row = lambda bi, i: (bi, i, 0)
    full2 = lambda bi, i: (0, 0)
    rows_out = [
        (SDS((b, l, 256), F32), 256), (SDS((b, l, 256), F32), 256), (SDS((b, l, 256), F32), 256),
        (SDS((b, l, 128), F32), 128), (SDS((b, l, ATT_WIDTH), BF16), ATT_WIDTH),
        (SDS((b, l, ATT_KV), BF16), ATT_KV),
    ]
    out_specs = [pl.BlockSpec((1, tm, w), row) for _, w in rows_out]
    out_shape = [s for s, _ in rows_out]
    out_specs.append(pl.BlockSpec((1, tm // tk, ATT_KV, tk), lambda bi, i: (bi, i, 0, 0)))
    out_shape.append(SDS((b, l // tk, ATT_KV, tk), BF16))
    out_specs.append(pl.BlockSpec((1, tm, 768), row))
    out_shape.append(SDS((b, l, 768), F32))
    return pl.pallas_call(
        functools.partial(_inproj_kernel, tk=tk),
        grid=(b, l // tm),
        in_specs=[
            pl.BlockSpec((1, tm, d), row),
            pl.BlockSpec((1, 6, d), lambda bi, i: (bi, 0, 0)),
            pl.BlockSpec((1, d), full2),
            pl.BlockSpec((d, _NPACK), full2),
            pl.BlockSpec((tm, LANES), lambda bi, i: (i, 0)),
            pl.BlockSpec((tm, LANES), lambda bi, i: (i, 0)),
            pl.BlockSpec((1, LANES), full2),
            pl.BlockSpec((1, LANES), full2),
        ],
        out_specs=out_specs,
        out_shape=out_shape,
        compiler_params=_params("parallel", "parallel"),
        name="in_proj",
    )(x, mod_l, g, w_pack, cos2, sin2, qg2, kg2)


def _attn_kernel(q_ref, k_ref, vt_ref, o_ref, acc_ref, s_ref, p_ref, *, tq):
    nchunk = vt_ref.shape[1]
    tk = vt_ref.shape[3]
    lane = lax.broadcasted_iota(jnp.int32, (tq, LANES), 1)
    lower = lane < HEAD_DIM
    ones_rows = ATT_ONES_ROWS
    vrows = HEAD_DIM + ones_rows
    row = lax.broadcasted_iota(jnp.int32, (vrows, tk), 0)
    q = q_ref[0].astype(F32)
    streams = []
    qts = []
    for j in range(ATT_KV_HEADS):
        keep = lower if j == 0 else jnp.logical_not(lower)
        for hp in range(ATT_GROUP // 2):
            cb = j * (ATT_GROUP // 2) + hp
            qc = q[:, cb * LANES:(cb + 1) * LANES]
            qsw = pltpu.roll(qc, HEAD_DIM, 1)
            first, second = (qc, qsw) if j == 0 else (qsw, qc)
            cols = [jnp.where(keep, first, 0.0).T, jnp.where(keep, second, 0.0).T]
            qts.append(jnp.concatenate(cols, axis=1).astype(BF16))
            streams.append((j, cb))
    qt_all = jnp.concatenate(qts, axis=1)
    half = ATT_GROUP * tq
    acc_ref[...] = jnp.zeros(acc_ref.shape, F32)
    p_ref[1] = jnp.zeros(p_ref.shape[1:], BF16)
    ones = jnp.ones((vrows, tk), BF16)

    def scores(c):
        kc = k_ref[0, pl.ds(pl.multiple_of(c * tk, tk), tk), :]
        return jnp.dot(kc, qt_all, preferred_element_type=F32)

    def add_values(c, slot, alpha):
        vt = vt_ref[0, c]
        for j in range(ATT_KV_HEADS):
            r0 = j * (LANES - vrows)
            vm = jnp.where((row < HEAD_DIM) if j == 0 else (row >= ones_rows), vt[r0:r0 + vrows, :], ones)
            pj = p_ref[slot, :, j * half:(j + 1) * half]
            acc_ref[j] = acc_ref[j] * alpha[:, j * half:(j + 1) * half] + jnp.dot(vm, pj, preferred_element_type=F32)

    s_ref[0] = scores(0)

    def step(c, slot, carry):
        m_prev, alpha_prev = carry
        s_ref[1 - slot] = scores(jnp.minimum(c + 1, nchunk - 1))
        add_values(jnp.maximum(c - 1, 0), 1 - slot, alpha_prev)
        st = s_ref[slot]
        m_new = jnp.maximum(m_prev, jnp.max(st, axis=0, keepdims=True))
        p_ref[slot] = jnp.exp((st - m_new).astype(BF16))
        return m_new, jnp.exp(m_prev - m_new)

    def body(i, carry):
        return step(2 * i + 1, 1, step(2 * i, 0, carry))

    assert nchunk % 2 == 0
    init = (jnp.full((1, 2 * half), -jnp.inf, F32), jnp.ones((1, 2 * half), F32))
    _, alpha = lax.fori_loop(0, nchunk // 2, body, init)
    add_values(nchunk - 1, 1, alpha)

    for j, cb in streams:
        hp = cb % (ATT_GROUP // 2)
        acc = acc_ref[j][:, hp * 2 * tq:(hp + 1) * 2 * tq]
        d0 = j * ones_rows
        denom = acc[(1 - j) * HEAD_DIM:(1 - j) * HEAD_DIM + 1, :]
        data = acc[d0:d0 + HEAD_DIM, :] / denom
        blk = jnp.concatenate([data[:, :tq], data[:, tq:]], axis=0)
        o_ref[0, :, cb * LANES:(cb + 1) * LANES] = blk.T.astype(BF16)


def _attn_call(q, k, vt):
    b, l, _ = q.shape
    tq = min(1024, l)
    nchunk, tk = vt.shape[1], vt.shape[3]
    nq = ATT_HEADS * tq
    return pl.pallas_call(
        functools.partial(_attn_kernel, tq=tq),
        grid=(b, l // tq),
        in_specs=[
            pl.BlockSpec((1, tq, ATT_WIDTH), lambda bi, i: (bi, i, 0)),
            pl.BlockSpec((1, l, ATT_KV), lambda bi, i: (bi, 0, 0)),
            pl.BlockSpec((1, nchunk, ATT_KV, tk), lambda bi, i: (bi, 0, 0, 0)),
        ],
        out_specs=pl.BlockSpec((1, tq, ATT_WIDTH), lambda bi, i: (bi, i, 0)),
        out_shape=SDS((b, l, ATT_WIDTH), BF16),
        scratch_shapes=[pltpu.VMEM((ATT_KV_HEADS, HEAD_DIM + ATT_ONES_ROWS, nq // ATT_KV_HEADS), F32),
                        pltpu.VMEM((2, tk, nq), F32), pltpu.VMEM((2, tk, nq), BF16)],
        compiler_params=_params("parallel", "parallel"),
        name="gqa_attention",
    )(q, k, vt)


def _log_sigmoid(x):
    return jnp.minimum(x, 0.0) - jnp.log(1.0 + jnp.exp(-jnp.abs(x)))


def _gla_kernel(qk_ref, v_ref, gg_ref, lr_ref, wdec_ref, bdec_ref, ng_ref, o_ref,
                qs_ref, ks_ref, a_ref, od_ref, *, rt):
    seq = qk_ref.shape[1]
    cb = GLA_BLOCK
    sb = GLA_STATE_BLOCK
    rowi = lax.broadcasted_iota(jnp.int32, (rt, GLA_QK), 0)
    rowmod = rowi % cb
    hd = lax.broadcasted_iota(jnp.int32, (GLA_QK, GLA_WIDTH), 0) // GLA_DK
    hc = lax.broadcasted_iota(jnp.int32, (GLA_QK, GLA_WIDTH), 1) // GLA_DV
    expand = jnp.where(hd == hc, 1.0, 0.0).astype(BF16)
    qk_head = lax.broadcasted_iota(jnp.int32, (rt, GLA_QK), 1) // GLA_DK
    v_head = lax.broadcasted_iota(jnp.int32, (rt, GLA_WIDTH), 1) // GLA_DV
    pair_i = lax.broadcasted_iota(jnp.int32, (rt, rt), 0)
    pair_j = lax.broadcasted_iota(jnp.int32, (rt, rt), 1)
    scale = GLA_DK ** -0.5

    def prep(t, _):
        r0 = pl.multiple_of(t * rt, rt)
        rows = pl.ds(r0, rt)
        q = qk_ref[0, rows, 0:GLA_QK] * scale
        k = qk_ref[0, rows, GLA_QK:2 * GLA_QK]
        v = v_ref[0, rows, :]
        lr = lr_ref[0, rows, :]
        od = jnp.zeros((rt, GLA_WIDTH), F32)
        half_sums = []
        for direction in range(2):
            la = _log_sigmoid(jnp.dot(lr, wdec_ref[direction], precision=HI, preferred_element_type=F32)
                              + bdec_ref[direction]) * (1.0 / GLA_GATE_NORM)
            pre = la
            suf = la
            s = 1
            while s < cb:
                pre = pre + jnp.where(rowmod >= s, pltpu.roll(pre, s, 0), 0.0)
                suf = suf + jnp.where(rowmod < cb - s, pltpu.roll(suf, rt - s, 0), 0.0)
                s *= 2
            tot = pre + suf - la
            sums = {}
            size = cb
            while size < sb:
                sums[size] = (pre, suf, la)
                upper = (rowi % (2 * size)) >= size
                t_below = pltpu.roll(tot, size, 0)
                t_above = pltpu.roll(tot, rt - size, 0)
                pre = pre + jnp.where(upper, t_below, 0.0)
                suf = suf + jnp.where(upper, 0.0, t_above)
                tot = tot + jnp.where(upper, t_below, t_above)
                size *= 2
            half_sums.append(sums)
            cum = pre if direction == 0 else suf
            rem = (suf if direction == 0 else pre) - la
            qs_ref[direction, rows, :] = q * jnp.exp(cum)
            ks_ref[direction, rows, :] = k * jnp.exp(rem)
            a_ref[direction, rows, :] = jnp.exp(tot)
            for delta in range(cb):
                if delta == 0:
                    a_mat = q * k
                    vj = v
                else:
                    if direction == 0:
                        sh = delta
                        valid = rowmod >= delta
                    else:
                        sh = rt - delta
                        valid = rowmod < cb - delta
                    kj = pltpu.roll(k, sh, 0)
                    cj = pltpu.roll(cum, sh, 0)
                    vj = pltpu.roll(v, sh, 0)
                    a_mat = jnp.where(valid, q * kj * jnp.exp(cum - cj), 0.0)
                se = jnp.dot(a_mat.astype(BF16), expand, preferred_element_type=F32)
                od = od + se * vj
        size = cb
        while size < sb:
            upper = (rowi % (2 * size)) >= size
            pre_f, suf_f, la_f = half_sums[0][size]
            pre_b, suf_b, la_b = half_sums[1][size]
            qf = (q * jnp.exp(jnp.where(upper, pre_f, suf_b))).astype(BF16)
            kf = (k * jnp.exp(jnp.where(upper, pre_b - la_b, suf_f - la_f))).astype(BF16)
            same_block = (pair_i // (2 * size)) == (pair_j // (2 * size))
            cross = ((pair_i % (2 * size)) >= size) != ((pair_j % (2 * size)) >= size)
            keep = jnp.logical_and(same_block, cross)
            for h in range(GLA_HEADS):
                qh = jnp.where(qk_head == h, qf, jnp.zeros_like(qf))
                sc = lax.dot_general(qh, kf, (((1,), (1,)), ((), ())), preferred_element_type=F32)
                vh = jnp.where(v_head == h, v, 0.0).astype(BF16)
                od = od + jnp.dot(jnp.where(keep, sc, 0.0).astype(BF16), vh, preferred_element_type=F32)
            size *= 2
        od_ref[rows, :] = od
        return 0

    lax.fori_loop(0, seq // rt, prep, 0)

    sr = lax.broadcasted_iota(jnp.int32, (GLA_WIDTH, GLA_QK), 0) // GLA_DV
    scol = lax.broadcasted_iota(jnp.int32, (GLA_WIDTH, GLA_QK), 1) // GLA_DK
    same_head = sr == scol
    nblk = seq // sb
    unroll = min(GLA_STEP_UNROLL, nblk)
    span = unroll * sb

    def step(i, states):
        states = list(states)
        for direction in range(2):
            base = i * span if direction == 0 else seq - (i + 1) * span
            rows = pl.ds(pl.multiple_of(base, span), span)
            qb = qs_ref[direction, rows, :].astype(BF16)
            kb = ks_ref[direction, rows, :].astype(BF16)
            vb = v_ref[0, rows, :].astype(BF16)
            ab = a_ref[direction, rows, :]
            st = states[direction]
            outs = [None] * unroll
            for u in (range(unroll) if direction == 0 else range(unroll - 1, -1, -1)):
                sl = slice(u * sb, (u + 1) * sb)
                outs[u] = lax.dot_general(qb[sl], st.astype(BF16), (((1,), (1,)), ((), ())),
                                          preferred_element_type=F32)
                kv = lax.dot_general(vb[sl], kb[sl], (((0,), (0,)), ((), ())), preferred_element_type=F32)
                st = st * ab[u * sb:u * sb + 1, :] + jnp.where(same_head, kv, 0.0)
            od_ref[rows, :] += jnp.concatenate(outs, axis=0)
            states[direction] = st
        return tuple(states)

    zero_state = jnp.zeros((GLA_WIDTH, GLA_QK), F32)
    lax.fori_loop(0, nblk // unroll, step, (zero_state, zero_state))

    gr = lax.broadcasted_iota(jnp.int32, (GLA_WIDTH, GLA_WIDTH), 0) // GLA_DV
    gc = lax.broadcasted_iota(jnp.int32, (GLA_WIDTH, GLA_WIDTH), 1) // GLA_DV
    gmat = jnp.where(gr == gc, 1.0 / GLA_DV, 0.0).astype(BF16)

    def fin(t, _):
        rows = pl.ds(pl.multiple_of(t * rt, rt), rt)
        o = od_ref[rows, :]
        ms = jnp.dot((o * o).astype(BF16), gmat, preferred_element_type=F32)
        o = o * lax.rsqrt(ms + EPS) * ng_ref[...]
        o_ref[0, rows, :] = (o * _silu(gg_ref[0, rows, :])).astype(BF16)
        return 0

    lax.fori_loop(0, seq // rt, fin, 0)


def _gla_call(gqk, gv, gg, glr, wdec_pad, bdec, ng4):
    b, l, _ = gqk.shape
    rt = min(256, l)
    once = dict(pipeline_mode=pl.Buffered(1))
    bmap = lambda bi: (bi, 0, 0)
    return pl.pallas_call(
        functools.partial(_gla_kernel, rt=rt),
        grid=(b,),
        in_specs=[
            pl.BlockSpec((1, l, 2 * GLA_QK), bmap, **once),
            pl.BlockSpec((1, l, GLA_WIDTH), bmap, **once),
            pl.BlockSpec((1, l, GLA_WIDTH), bmap, **once),
            pl.BlockSpec((1, l, LANES), bmap, **once),
            pl.BlockSpec((2, LANES, GLA_QK), lambda bi: (0, 0, 0)),
            pl.BlockSpec((2, 1, GLA_QK), lambda bi: (0, 0, 0)),
            pl.BlockSpec((1, GLA_WIDTH), lambda bi: (0, 0)),
        ],
        out_specs=pl.BlockSpec((1, l, GLA_WIDTH), bmap),
        out_shape=SDS((b, l, GLA_WIDTH), BF16),
        scratch_shapes=[
            pltpu.VMEM((2, l, GLA_QK), F32),
            pltpu.VMEM((2, l, GLA_QK), F32),
            pltpu.VMEM((2, l, GLA_QK), F32),
            pltpu.VMEM((l, GLA_WIDTH), F32),
        ],
        compiler_params=_params("parallel"),
        name="gla",
    )(gqk, gv, gg, glr, wdec_pad, bdec, ng4)


def _hy_pre_kernel(x0_ref, x1_ref, v_ref, w0_ref, w1_ref, wv_ref, b0_ref, b1_ref, bv_ref, z_ref, x0o_ref, *, rt):
    seq = x0_ref.shape[1]
    rowid = lax.broadcasted_iota(jnp.int32, (rt, LANES), 0)

    def conv(ref, w_ref, b_ref, r0):
        cur = ref[0, pl.ds(r0, rt), :]
        prev = ref[0, pl.ds(pl.multiple_of(jnp.maximum(r0 - 8, 0), 8), 8), :][7:8, :]
        nxt = ref[0, pl.ds(pl.multiple_of(jnp.minimum(r0 + rt, seq - 8), 8), 8), :][0:1, :]
        prev = jnp.where(r0 > 0, prev, 0.0)
        nxt = jnp.where(r0 + rt < seq, nxt, 0.0)
        up = jnp.where(rowid == 0, prev, pltpu.roll(cur, 1, 0))
        down = jnp.where(rowid == rt - 1, nxt, pltpu.roll(cur, rt - 1, 0))
        return up * w_ref[0:1, :] + cur * w_ref[1:2, :] + down * w_ref[2:3, :] + b_ref[...]

    def body(t, _):
        r0 = pl.multiple_of(t * rt, rt)
        rows = pl.ds(r0, rt)
        x0o_ref[0, rows, :] = conv(x0_ref, w0_ref, b0_ref, r0)
        z_ref[0, rows, :] = conv(v_ref, wv_ref, bv_ref, r0) * conv(x1_ref, w1_ref, b1_ref, r0)
        return 0

    lax.fori_loop(0, seq // rt, body, 0)


def _hy_pre_call(hy, conv_w, conv_b):
    b, l, w3 = hy.shape
    hw = w3 // 3
    nj = hw // LANES
    rt = min(512, l)
    xs = lambda off: pl.BlockSpec((1, l, LANES), lambda bi, j: (bi, 0, off * nj + j))
    ws = lambda off: pl.BlockSpec((3, LANES), lambda bi, j: (0, off * nj + j))
    bs = lambda off: pl.BlockSpec((1, LANES), lambda bi, j: (0, off * nj + j))
    out = pl.BlockSpec((1, l, LANES), lambda bi, j: (bi, 0, j))
    return pl.pallas_call(
        functools.partial(_hy_pre_kernel, rt=rt),
        grid=(b, nj),
        in_specs=[xs(0), xs(1), xs(2), ws(0), ws(1), ws(2), bs(0), bs(1), bs(2)],
        out_specs=[out, out],
        out_shape=[SDS((b, l, hw), F32), SDS((b, l, hw), F32)],
        compiler_params=_params("parallel", "parallel"),
        name="hyena_short_conv",
    )(hy, hy, hy, conv_w, conv_w, conv_w, conv_b, conv_b, conv_b)


def _hy_filter_kernel(z_ref, w1_ref, b1_ref, w2_ref, b2_ref, w3_ref, fr_ref, dl_ref, o_ref, *, seq, tr):
    half = tr // 2
    zs = (z_ref[0:half, :], z_ref[half:tr, :])
    fr = fr_ref[...]
    pre = (jnp.dot(zs[0], w1_ref[0], precision=HI, preferred_element_type=F32)
           + jnp.dot(zs[1], w1_ref[1], precision=HI, preferred_element_type=F32))
    h = jnp.sin(fr * (pre + b1_ref[...]))
    for i in range(w2_ref.shape[0]):
        h = jnp.sin(fr * (jnp.dot(h, w2_ref[i], precision=HI, preferred_element_type=F32) + b2_ref[i]))
    c = dl_ref.shape[1]
    for part in range(2):
        hw = jnp.dot(h, w3_ref[part], precision=HI, preferred_element_type=F32)
        window = jnp.exp(-zs[part][:, 0:1] * dl_ref[...])
        m = pl.program_id(0) * tr + part * half + lax.broadcasted_iota(jnp.int32, (half, c), 0)
        tap = jnp.where(m >= seq, hw[:, :c], hw[:, c:]) * window
        o_ref[part * half:(part + 1) * half, :] = jnp.where(m == 0, 0.0, tap)


def _hy_filter_call(feat, w1p, b1, w2, b2, w3, freq, deltas, seq):
    n2, _ = feat.shape
    od = w1p.shape[1]
    c2 = w3.shape[1]
    tr = min(1024, n2)
    zero = lambda a: jnp.zeros_like(a)
    w1_2 = jnp.stack([jnp.concatenate([w1p, zero(w1p)], 1), jnp.concatenate([zero(w1p), w1p], 1)])
    w2_2 = jnp.concatenate([jnp.concatenate([w2, zero(w2)], 2), jnp.concatenate([zero(w2), w2], 2)], 1)
    w3_2 = jnp.stack([jnp.concatenate([w3, zero(w3)], 0), jnp.concatenate([zero(w3), w3], 0)])
    twice = lambda a: jnp.concatenate([a, a], -1)
    full = lambda *shape: pl.BlockSpec(shape, lambda i: (0,) * len(shape))
    return pl.pallas_call(
        functools.partial(_hy_filter_kernel, seq=seq, tr=tr),
        grid=(n2 // tr,),
        in_specs=[
            pl.BlockSpec((tr, LANES), lambda i: (i, 0)),
            full(2, LANES, 2 * od), full(1, 2 * od), full(w2.shape[0], 2 * od, 2 * od), full(w2.shape[0], 1, 2 * od),
            full(2, 2 * od, c2), full(1, 2 * od), full(1, c2 // 2),
        ],
        out_specs=pl.BlockSpec((tr, c2 // 2), lambda i: (i, 0)),
        out_shape=SDS((n2, c2 // 2), F32),
        compiler_params=_params("parallel"),
        name="hyena_filter",
    )(feat, w1_2, twice(b1), w2_2, twice(b2), w3_2, twice(freq), deltas)


def _hy_conv_kernel(z_ref, f_ref, y_ref, acc_ref, *, nb, bp):
    cg = z_ref.shape[0]
    p = HY_BLOCK
    ii = lax.broadcasted_iota(jnp.int32, (p, p), 0)
    jj = lax.broadcasted_iota(jnp.int32, (p, p), 1)
    upper = jj >= ii

    rows = nb * bp
    sub = 8
    shifts = sorted({(bp * d) % sub for d in range(-(nb - 1), nb)})

    def chan(ci, _):
        pad = jnp.zeros((sub, p), F32)
        zfull = jnp.concatenate([pad, z_ref[ci], pad], axis=0)
        zs = {m: (zfull if m == 0 else pltpu.roll(zfull, m, 0)) for m in shifts}
        acc_ref[...] = jnp.zeros(acc_ref.shape, F32)

        def skew(k):
            taps = jnp.broadcast_to(f_ref[ci, k:k + 1, :], (p, p))
            return pltpu.roll(taps, 0, 1, stride=1, stride_axis=0).astype(BF16)

        hi = skew(0)
        for k in range(2 * nb - 1):
            delta = k - (nb - 1)
            lo, hi = hi, skew(k + 1)
            w = jnp.where(upper, hi, lo)
            s = bp * delta
            o0 = max(0, s) // sub * sub
            o1 = -(-min(rows, rows + s) // sub) * sub
            m = s % sub
            src = o0 - s + sub + m
            lhs = zs[m][src:src + (o1 - o0), :].astype(BF16)
            acc_ref[o0:o1, :] += jnp.dot(lhs, w, preferred_element_type=F32)
        y_ref[ci] = acc_ref[...]
        return 0

    lax.fori_loop(0, cg, chan, 0)


def _hy_conv_call(zt, ft, nb, bp):
    c, rows, p = zt.shape
    cg = 8
    return pl.pallas_call(
        functools.partial(_hy_conv_kernel, nb=nb, bp=bp),
        grid=(c // cg,),
        in_specs=[
            pl.BlockSpec((cg, rows, p), lambda g: (g, 0, 0)),
            pl.BlockSpec((cg, 2 * nb, p), lambda g: (g, 0, 0)),
        ],
        out_specs=pl.BlockSpec((cg, rows, p), lambda g: (g, 0, 0)),
        out_shape=SDS((c, rows, p), F32),
        scratch_shapes=[pltpu.VMEM((rows, p), F32)],
        compiler_params=_params("parallel"),
        name="hyena_long_conv",
    )(zt, ft)


def _outproj_kernel(x_ref, ogla_ref, oatt_ref, yc_ref, z_ref, x0_ref, skip_ref, w_ref, mod_ref,
                    gpost_ref, gpre_ref, rw_ref, xo_ref, h2_ref, aff_ref, *, n_exp, sub):
    g0 = GLA_WIDTH
    g1 = GLA_WIDTH + ATT_WIDTH
    g_m = mod_ref[0, 2:3, :]
    sh_f = mod_ref[0, 3:4, :]
    sc_f = mod_ref[0, 4:5, :]
    for s in range(x_ref.shape[1] // sub):
        rows = slice(s * sub, (s + 1) * sub)
        ohy = (x0_ref[0, rows, :] * (yc_ref[0, rows, :] + z_ref[0, rows, :] * skip_ref[...])).astype(BF16)
        y = jnp.dot(ogla_ref[0, rows, :], w_ref[0:g0, :], preferred_element_type=F32)
        y = y + jnp.dot(oatt_ref[0, rows, :], w_ref[g0:g1, :], preferred_element_type=F32)
        y = y + jnp.dot(ohy, w_ref[g1:, :], preferred_element_type=F32)
        x1 = x_ref[0, rows, :] + g_m * (_rms(y) * gpost_ref[...])
        xo_ref[0, rows, :] = x1
        h2 = _rms(x1) * gpre_ref[...] * (1.0 + sc_f) + sh_f
        h2_ref[0, :, rows] = h2.T.astype(BF16)
        h_hi = h2.astype(BF16)
        h_lo = (h2 - h_hi.astype(F32)).astype(BF16)
        logits = (jnp.dot(h_hi, rw_ref[0], preferred_element_type=F32)
                  + jnp.dot(h_lo, rw_ref[0], preferred_element_type=F32)
                  + jnp.dot(h_hi, rw_ref[1], preferred_element_type=F32))
        lane = lax.broadcasted_iota(jnp.int32, logits.shape, 1)
        logits = jnp.where(lane < n_exp, logits, -jnp.inf)
        e = jnp.exp(logits - jnp.max(logits, axis=-1, keepdims=True))
        aff_ref[0, rows, :] = e / jnp.sum(e, axis=-1, keepdims=True)


def _outproj_call(x, ogla, oatt, yc, z, x0, skip, w_out, mod_l, gpost, gpre, rw_pad, n_exp):
    b, l, d = x.shape
    tm = min(1024, l)
    row = lambda bi, i: (bi, i, 0)
    full2 = lambda bi, i: (0, 0)
    hw = yc.shape[-1]
    return pl.pallas_call(
        functools.partial(_outproj_kernel, n_exp=n_exp, sub=min(256, tm)),
        grid=(b, l // tm),
        in_specs=[
            pl.BlockSpec((1, tm, d), row),
            pl.BlockSpec((1, tm, GLA_WIDTH), row),
            pl.BlockSpec((1, tm, ATT_WIDTH), row),
            pl.BlockSpec((1, tm, hw), row),
            pl.BlockSpec((1, tm, hw), row),
            pl.BlockSpec((1, tm, hw), row),
            pl.BlockSpec((1, hw), full2),
            pl.BlockSpec((d, d), full2),
            pl.BlockSpec((1, 6, d), lambda bi, i: (bi, 0, 0)),
            pl.BlockSpec((1, d), full2),
            pl.BlockSpec((1, d), full2),
            pl.BlockSpec((2, d, LANES), lambda bi, i: (0, 0, 0)),
        ],
        out_specs=[pl.BlockSpec((1, tm, d), row), pl.BlockSpec((1, d, tm), lambda bi, i: (bi, 0, i)),
                   pl.BlockSpec((1, tm, LANES), row)],
        out_shape=[SDS((b, l, d), F32), SDS((b, d, l), BF16), SDS((b, l, LANES), F32)],
        compiler_params=_params("parallel", "parallel"),
        name="out_proj_router",
    )(x, ogla, oatt, yc, z, x0, skip, w_out, mod_l, gpost, gpre, rw_pad)


def _route_kernel(aff_ref, pos_ref, cum_ref, *, cap):
    aff = aff_ref[0]
    n_exp, t = aff.shape

    def bisect(_, bounds):
        lo, hi = bounds
        mid = 0.5 * (lo + hi)
        ok = jnp.sum(jnp.where(aff >= mid, 1.0, 0.0), axis=1, keepdims=True) >= cap
        return jnp.where(ok, mid, lo), jnp.where(ok, hi, mid)

    lo, hi = lax.fori_loop(0, ROUTE_BISECTIONS, bisect,
                           (jnp.zeros((n_exp, 1), F32), jnp.full((n_exp, 1), 2.0, F32)))
    need = cap - jnp.sum(jnp.where(aff >= hi, 1.0, 0.0), axis=1, keepdims=True)
    r = lax.broadcasted_iota(jnp.int32, (LANES, LANES), 0)
    c = lax.broadcasted_iota(jnp.int32, (LANES, LANES), 1)
    tri = jnp.where(r <= c, 1.0, 0.0).astype(BF16)
    eq_run = jnp.zeros((n_exp, 1), F32)
    sel_run = jnp.zeros((n_exp, 1), F32)
    runs = []
    for j in range(t // LANES):
        sl = slice(j * LANES, (j + 1) * LANES)
        aj = aff[:, sl]
        eq_j = jnp.where(aj >= hi, 0.0, jnp.where(aj >= lo, 1.0, 0.0))
        eq_cum = jnp.dot(eq_j.astype(BF16), tri, preferred_element_type=F32) + eq_run
        sel_j = jnp.where(aj >= hi, 1.0, jnp.where(eq_cum <= need, eq_j, 0.0))
        sel_cum = jnp.dot(sel_j.astype(BF16), tri, preferred_element_type=F32) + sel_run
        pos_ref[0, :, sl] = jnp.where(sel_j > 0.0, sel_cum - 1.0, -1.0).astype(jnp.int32)
        eq_run = eq_cum[:, LANES - 1:LANES]
        sel_run = sel_cum[:, LANES - 1:LANES]
        runs.append(sel_run)
    cum_ref[0] = jnp.concatenate(runs, axis=1).astype(jnp.int32)


def _route_call(aff_t, cap):
    b, n_exp, t = aff_t.shape
    rows = b * n_exp
    pos, cum = pl.pallas_call(
        functools.partial(_route_kernel, cap=cap),
        grid=(1,),
        in_specs=[pl.BlockSpec((1, rows, t), lambda i: (0, 0, 0))],
        out_specs=[pl.BlockSpec((1, rows, t), lambda i: (0, 0, 0)),
                   pl.BlockSpec((1, rows, t // LANES), lambda i: (0, 0, 0))],
        out_shape=[SDS((1, rows, t), jnp.int32), SDS((1, rows, t // LANES), jnp.int32)],
        compiler_params=_params("arbitrary"),
        name="ec_route",
    )(aff_t.reshape(1, rows, t))
    return pos.reshape(rows, 1, t), cum.reshape(rows, t // LANES)


def _chunk_ranges(cum, cap, tc):
    per = tc // LANES
    cum_c = cum[:, per - 1::per]
    first = jnp.arange(0, cap, EC_SLOT_BLOCK, dtype=jnp.int32)
    chunk_of = lambda s: jnp.sum(cum_c[:, None, :] <= s[None, :, None], axis=-1).astype(jnp.int32)
    return chunk_of(first).reshape(-1), chunk_of(first + EC_SLOT_BLOCK - 1).reshape(-1)


def _one_hot(pos_ref, off, tc, first_slot):
    slot = first_slot + lax.broadcasted_iota(jnp.int32, (EC_SLOT_BLOCK, tc), 0)
    return jnp.where(slot == pos_ref[0, :, pl.ds(off, tc)], 1.0, 0.0).astype(BF16)


def _dispatch_kernel(lo_ref, hi_ref, pos_ref, aff_ref, ht_ref, xgt_ref, gc_ref, acc_ref, gacc_ref, *, tc, no_slot):
    nr = pl.num_programs(2)
    r = pl.program_id(2)
    idx = (pl.program_id(0) * pl.num_programs(1) + pl.program_id(1)) * nr + r
    acc_ref[...] = jnp.zeros(acc_ref.shape, F32)
    gacc_ref[...] = jnp.zeros(gacc_ref.shape, F32)
    nt = (((1,), (1,)), ((), ()))

    lo = lo_ref[idx]
    hi = hi_ref[idx]

    def body(i, _):
        c0 = lo + 2 * i
        chunks = ((c0, r * EC_SLOT_BLOCK),
                  (jnp.minimum(c0 + 1, hi), jnp.where(c0 + 1 <= hi, r * EC_SLOT_BLOCK, no_slot)))
        part = gpart = None
        for c, first_slot in chunks:
            off = pl.multiple_of(c * tc, tc)
            sel = _one_hot(pos_ref, off, tc, first_slot)
            d = lax.dot_general(ht_ref[0, :, pl.ds(off, tc)], sel, nt, preferred_element_type=F32)
            g = aff_ref[0, :, pl.ds(off, tc)]
            g1 = g.astype(BF16).astype(F32)
            g2 = (g - g1).astype(BF16).astype(F32)
            g3 = g - g1 - g2
            prow = lax.broadcasted_iota(jnp.int32, (gacc_ref.shape[0], tc), 0)
            pieces = jnp.where(prow == 0, g1, jnp.where(prow == 1, g2, jnp.where(prow == 2, g3, 0.0))).astype(BF16)
            gd = lax.dot_general(pieces, sel, nt, preferred_element_type=F32)
            part = d if part is None else part + d
            gpart = gd if gpart is None else gpart + gd
        acc_ref[...] += part
        gacc_ref[...] += gpart
        return 0

    lax.fori_loop(0, (hi - lo + 2) // 2, body, 0)
    xgt_ref[0] = acc_ref[...].astype(BF16)
    gc_ref[0] = jnp.sum(gacc_ref[...], axis=0, keepdims=True)


def _dispatch_call(lo, hi, pos_rows, aff_rows, h2t, n_exp, cap):
    b, d, t = h2t.shape
    tc = min(EC_TOKEN_CHUNK, t)
    nr = cap // EC_SLOT_BLOCK
    row = lambda bi, e, r, lo_r, hi_r: (bi * n_exp + e, 0, 0)
    out = lambda bi, e, r, lo_r, hi_r: (e, 0, bi * nr + r)
    return pl.pallas_call(
        functools.partial(_dispatch_kernel, tc=tc, no_slot=cap),
        grid_spec=pltpu.PrefetchScalarGridSpec(
            num_scalar_prefetch=2,
            grid=(b, n_exp, nr),
            in_specs=[
                pl.BlockSpec((1, 1, t), row),
                pl.BlockSpec((1, 1, t), row),
                pl.BlockSpec((1, d, t), lambda bi, e, r, lo_r, hi_r: (bi, 0, 0)),
            ],
            out_specs=[pl.BlockSpec((1, d, EC_SLOT_BLOCK), out), pl.BlockSpec((1, 1, EC_SLOT_BLOCK), out)],
            scratch_shapes=[pltpu.VMEM((d, EC_SLOT_BLOCK), F32), pltpu.VMEM((16, EC_SLOT_BLOCK), F32)],
        ),
        out_shape=[SDS((n_exp, d, b * cap), BF16), SDS((n_exp, 1, b * cap), F32)],
        compiler_params=_params("parallel", "arbitrary", "arbitrary"),
        name="ec_dispatch",
    )(lo, hi, pos_rows, aff_rows, h2t)


def _ffn_kernel(xgt_ref, gc_ref, wg_ref, wu_ref, wd_ref, yt_ref, xg_ref, acc_ref, *, mt):
    f = pl.program_id(1)
    m = xg_ref.shape[0]

    @pl.when(f == 0)
    def _():
        acc_ref[...] = jnp.zeros(acc_ref.shape, F32)
        for i in range(m // mt):
            xg_ref[i * mt:(i + 1) * mt, :] = xgt_ref[0, :, i * mt:(i + 1) * mt].T

    wg = wg_ref[0, 0].astype(BF16)
    wu = wu_ref[0, 0].astype(BF16)
    wd = wd_ref[0, 0].astype(BF16)
    for i in range(m // mt):
        rows = slice(i * mt, (i + 1) * mt)
        xb = xg_ref[rows, :]
        a = jnp.dot(xb, wg, preferred_element_type=F32)
        u = jnp.dot(xb, wu, preferred_element_type=F32)
        acc_ref[rows, :] += jnp.dot((_silu(a) * u).astype(BF16), wd, preferred_element_type=F32)

    @pl.when(f == pl.num_programs(1) - 1)
    def _():
        for i in range(m // mt):
            cols = slice(i * mt, (i + 1) * mt)
            yt_ref[0, :, cols] = (acc_ref[cols, :].T * gc_ref[0, :, cols]).astype(BF16)


def _ffn_call(xgt, gc, w_gate, w_up, w_down, layer):
    n_exp, d, m = xgt.shape
    ff = w_gate.shape[3]
    tf = 256
    mt = min(512, m)
    return pl.pallas_call(
        functools.partial(_ffn_kernel, mt=mt),
        grid=(n_exp, ff // tf),
        in_specs=[
            pl.BlockSpec((1, d, m), lambda e, f: (e, 0, 0)),
            pl.BlockSpec((1, 1, m), lambda e, f: (e, 0, 0)),
            pl.BlockSpec((1, 1, d, tf), lambda e, f: (layer, e, 0, f)),
            pl.BlockSpec((1, 1, d, tf), lambda e, f: (layer, e, 0, f)),
            pl.BlockSpec((1, 1, tf, d), lambda e, f: (layer, e, f, 0)),
        ],
        out_specs=pl.BlockSpec((1, d, m), lambda e, f: (e, 0, 0)),
        out_shape=SDS((n_exp, d, m), BF16),
        scratch_shapes=[pltpu.VMEM((m, d), BF16), pltpu.VMEM((m, d), F32)],
        compiler_params=_params("parallel", "arbitrary"),
        name="expert_ffn",
    )(xgt, gc, w_gate, w_up, w_down)


def _combine_kernel(lo_ref, hi_ref, pos_ref, yt_ref, o_ref, *, tc, no_slot):
    ne = pl.num_programs(1)
    nr = pl.num_programs(2)
    e = pl.program_id(1)
    r = pl.program_id(2)
    idx = (pl.program_id(0) * ne + e) * nr + r

    @pl.when(jnp.logical_and(e == 0, r == 0))
    def _():
        o_ref[...] = jnp.zeros(o_ref.shape, F32)

    yb = yt_ref[0]
    lo = lo_ref[idx]
    hi = hi_ref[idx]

    def body(i, _):
        c0 = lo + 2 * i
        chunks = ((c0, r * EC_SLOT_BLOCK),
                  (jnp.minimum(c0 + 1, hi), jnp.where(c0 + 1 <= hi, r * EC_SLOT_BLOCK, no_slot)))
        adds = []
        for c, first_slot in chunks:
            off = pl.multiple_of(c * tc, tc)
            adds.append((off, jnp.dot(yb, _one_hot(pos_ref, off, tc, first_slot), preferred_element_type=F32)))
        for off, contrib in adds:
            o_ref[0, :, pl.ds(off, tc)] += contrib
        return 0

    lax.fori_loop(0, (hi - lo + 2) // 2, body, 0)


def _combine_call(lo, hi, pos_rows, yt, b, t, cap):
    n_exp, d, _ = yt.shape
    tc = min(EC_TOKEN_CHUNK, t)
    nr = cap // EC_SLOT_BLOCK
    return pl.pallas_call(
        functools.partial(_combine_kernel, tc=tc, no_slot=cap),
        grid_spec=pltpu.PrefetchScalarGridSpec(
            num_scalar_prefetch=2,
            grid=(b, n_exp, nr),
            in_specs=[
                pl.BlockSpec((1, 1, t), lambda bi, e, r, lo_r, hi_r: (bi * n_exp + e, 0, 0)),
                pl.BlockSpec((1, d, EC_SLOT_BLOCK), lambda bi, e, r, lo_r, hi_r: (e, 0, bi * nr + r)),
            ],
            out_specs=pl.BlockSpec((1, d, t), lambda bi, e, r, lo_r, hi_r: (bi, 0, 0)),
        ),
        out_shape=SDS((b, d, t), F32),
        compiler_params=_params("parallel", "arbitrary", "arbitrary"),
        name="ec_combine",
    )(lo, hi, pos_rows, yt)


def _ffn_residual_kernel(acc_ref, x_ref, mod_ref, g_ref, o_ref):
    y = acc_ref[0].T
    g_f = mod_ref[0, 5:6, :]
    o_ref[0] = x_ref[0] + g_f * (_rms(y) * g_ref[...])


def _ffn_residual_call(acc_t, x, mod_l, gpost):
    b, t, d = x.shape
    tt = min(512, t)
    return pl.pallas_call(
        _ffn_residual_kernel,
        grid=(b, t // tt),
        in_specs=[
            pl.BlockSpec((1, d, tt), lambda bi, i: (bi, 0, i)),
            pl.BlockSpec((1, tt, d), lambda bi, i: (bi, i, 0)),
            pl.BlockSpec((1, 6, d), lambda bi, i: (bi, 0, 0)),
            pl.BlockSpec((1, d), lambda bi, i: (0, 0)),
        ],
        out_specs=pl.BlockSpec((1, tt, d), lambda bi, i: (bi, i, 0)),
        out_shape=SDS((b, t, d), F32),
        compiler_params=_params("parallel", "parallel"),
        name="ffn_residual",
    )(acc_t, x, mod_l, gpost)


def _rope_tables(seq):
    n_rows = seq // GRID_W
    row = jnp.repeat(jnp.arange(n_rows), GRID_W).astype(F32)
    col = jnp.tile(jnp.arange(GRID_W), n_rows).astype(F32)
    inv = 1.0 / (ROPE_THETA ** (jnp.arange(0, ROPE_HALF, 2, dtype=F32) / ROPE_HALF))
    ar = row[:, None] * inv[None]
    ac = col[:, None] * inv[None]
    ang = jnp.concatenate([ar, ar, ac, ac], axis=-1)
    cos, sin = jnp.cos(ang), jnp.sin(ang)
    first_half = (jnp.arange(HEAD_DIM) % ROPE_HALF) < (ROPE_HALF // 2)
    sin_signed = jnp.where(first_half[None], -sin, sin)
    return jnp.tile(cos, (1, 2)), jnp.tile(sin_signed, (1, 2))


def _filter_features(seq):
    pos = jnp.abs(jnp.arange(2 * seq) - seq).astype(F32)
    t = pos / (seq - 1)
    f = jnp.linspace(1e-4, HY_BANDS - 1, HY_BANDS, dtype=F32)
    ang = (2.0 * math.pi * pos / seq)[:, None] * f[None]
    z = jnp.concatenate([t[:, None], jnp.cos(ang), -jnp.sin(ang)], axis=-1)
    return jnp.pad(z, ((0, 0), (0, LANES - HY_EMB)))


def kernel(x, c, ada_w, ada_b, mix_pre_g, mix_post_g, w_in, gla_w_dec, gla_b_dec, gla_norm_g, q_norm_g, k_norm_g, hy_conv_w, hy_conv_b, hy_w1, hy_b1, hy_w2, hy_b2, hy_w3, hy_freq, hy_skip, w_out, ffn_pre_g, ffn_post_g, router_w, exp_w_gate, exp_w_up, exp_w_down):
    b, seq, d = x.shape
    depth = ada_w.shape[0]
    n_exp = router_w.shape[-1]
    cap = EC_FACTOR * seq // n_exp
    hw = hy_skip.shape[-1]
    nb = seq // HY_BLOCK
    assert seq % HY_BLOCK == 0 and (nb * b) % 8 == 0

    mod = _ada_call(c, ada_w, ada_b).reshape(depth, b, 6, d)
    cos2, sin2 = _rope_tables(seq)
    feat = _filter_features(seq)
    deltas = jnp.abs(jnp.linspace(HY_MIN_DECAY, HY_MAX_DECAY, hw, dtype=F32))[None]

    for l in range(depth):
        w = w_in[l]
        w_pack = jnp.concatenate(
            [w[:, :768], jnp.pad(w[:, 768:800], ((0, 0), (0, LANES - 2 * GLA_LOWRANK))), w[:, 800:]], axis=1
        ).astype(BF16)
        gqk, gv, gg, glr, aq, ak, av, hy = _inproj_call(
            x, mod[l], mix_pre_g[l][None], w_pack, cos2, sin2,
            jnp.tile(q_norm_g[l], 2)[None], jnp.tile(k_norm_g[l], 2)[None])
        o_att = _attn_call(aq, ak, av)

        wdec_pad = jnp.zeros((2, LANES, GLA_QK), F32)
        wdec_pad = wdec_pad.at[0, :GLA_LOWRANK].set(gla_w_dec[l, 0]).at[1, GLA_LOWRANK:2 * GLA_LOWRANK].set(gla_w_dec[l, 1])
        o_gla = _gla_call(gqk, gv, gg, glr, wdec_pad, gla_b_dec[l][:, None, :],
                          jnp.tile(gla_norm_g[l], GLA_HEADS)[None])

        z, x0 = _hy_pre_call(hy, hy_conv_w[l], hy_conv_b[l][None])
        w1p = jnp.pad(hy_w1[l], ((0, LANES - HY_EMB), (0, 0)))
        filt = _hy_filter_call(feat, w1p, hy_b1[l][None], hy_w2[l], hy_b2[l][:, None, :], hy_w3[l],
                               hy_freq[l][None], deltas, seq)
        ft = filt.T.reshape(hw, 2 * nb, HY_BLOCK)
        zt = jnp.transpose(z.reshape(b, nb, HY_BLOCK, hw), (3, 1, 0, 2)).reshape(hw, nb * b, HY_BLOCK)
        yt = _hy_conv_call(zt, ft, nb, b).reshape(hw, nb, b, HY_BLOCK)
        yc = jnp.transpose(yt, (2, 1, 3, 0)).reshape(b, seq, hw)

        rw_full = jnp.pad(router_w[l], ((0, 0), (0, LANES - n_exp)))
        rw_hi = rw_full.astype(BF16)
        rw_pad = jnp.stack([rw_hi, (rw_full - rw_hi.astype(F32)).astype(BF16)])
        x, h2, aff = _outproj_call(x, o_gla, o_att, yc, z, x0, hy_skip[l][None], w_out[l].astype(BF16), mod[l],
                                   mix_post_g[l][None], ffn_pre_g[l][None], rw_pad, n_exp)

        aff_t = jnp.swapaxes(aff[..., :n_exp], 1, 2)
        pos_rows, cum = _route_call(aff_t, cap)
        lo, hi = _chunk_ranges(cum, cap, min(EC_TOKEN_CHUNK, seq))
        aff_rows = aff_t.reshape(b * n_exp, 1, seq)
        xgt, gc = _dispatch_call(lo, hi, pos_rows, aff_rows, h2, n_exp, cap)
        yt = _ffn_call(xgt, gc, exp_w_gate, exp_w_up, exp_w_down, l)
        acc_t = _combine_call(lo, hi, pos_rows, yt, b, seq, cap)
        x = _ffn_residual_call(acc_t, x, mod[l], ffn_post_g[l][None])
    return x
```

```python
import functools
import math

import jax
import jax.numpy as jnp
from jax import lax
from jax.experimental import pallas as pl
from jax.experimental.pallas import tpu as pltpu

F32 = jnp.float32
BF16 = jnp.bfloat16
HI = lax.Precision.HIGHEST
SDS = jax.ShapeDtypeStruct

EPS = 1e-6
GRID_W = 64
HEAD_DIM = 64
GLA_HEADS = 4
GLA_DK = 32
GLA_DV = 64
GLA_QK = GLA_HEADS * GLA_DK
GLA_WIDTH = GLA_HEADS * GLA_DV
GLA_LOWRANK = 16
GLA_GATE_NORM = 16.0
GLA_BLOCK = 8
GLA_STATE_BLOCK = 64
GLA_STEP_UNROLL = 4
ATT_HEADS = 8
ATT_KV_HEADS = 2
ATT_GROUP = ATT_HEADS // ATT_KV_HEADS
ATT_WIDTH = ATT_HEADS * HEAD_DIM
ATT_KV = ATT_KV_HEADS * HEAD_DIM
ATT_TK = 256
ATT_ONES_ROWS = 16
ROPE_THETA = 10000.0
ROPE_HALF = HEAD_DIM // 2
HY_EMB = 33
HY_BANDS = (HY_EMB - 1) // 2
HY_TARGET = 1e-2
HY_FAST = 0.3
HY_SLOW = 1.5
HY_MIN_DECAY = math.log(HY_TARGET) / HY_SLOW
HY_MAX_DECAY = math.log(HY_TARGET) / HY_FAST
HY_BLOCK = 128
EC_FACTOR = 2
ROUTE_BISECTIONS = 160
EC_SLOT_BLOCK = 256
EC_TOKEN_CHUNK = 512
LANES = 128
SUBLANES = 8
VMEM_LIMIT = 56 * 1024 * 1024

PROJ_ROWS = 1024
OUT_SUB_ROWS = 256
ADA_COLS = 1024
ATT_TQ = 1024
GLA_ROWS = 256
HY_PRE_ROWS = 512
HY_FILTER_ROWS = 1024
HY_CONV_CHANNELS = 8
FFN_COLS = 256
FFN_ROWS = 512
RESID_ROWS = 512


def _params(*sem):
    return pltpu.CompilerParams(dimension_semantics=sem, vmem_limit_bytes=VMEM_LIMIT)


def _silu(x):
    return x * (1.0 / (1.0 + jnp.exp(-x)))


def _rms(x):
    return x * lax.rsqrt(jnp.mean(x * x, axis=-1, keepdims=True) + EPS)


def _ada_kernel(c_ref, w_ref, b_ref, o_ref):
    cond = _silu(c_ref[...])
    o_ref[0] = jnp.dot(cond, w_ref[0], precision=HI, preferred_element_type=F32) + b_ref[0]


def _ada_call(c, ada_w, ada_b):
    depth, d, n6 = ada_w.shape
    b = c.shape[0]
    tn = min(ADA_COLS, n6)
    return pl.pallas_call(
        _ada_kernel,
        grid=(depth, n6 // tn),
        in_specs=[
            pl.BlockSpec((b, d), lambda l, j: (0, 0)),
            pl.BlockSpec((1, d, tn), lambda l, j: (l, 0, j)),
            pl.BlockSpec((1, 1, tn), lambda l, j: (l, 0, j)),
        ],
        out_specs=pl.BlockSpec((1, b, tn), lambda l, j: (l, 0, j)),
        out_shape=SDS((depth, b, n6), F32),
        compiler_params=_params("arbitrary", "arbitrary"),
        name="ada_mod",
    )(c, ada_w, ada_b.reshape(depth, 1, n6))


_C_GQK = (0, 256)
_C_GV = (256, 512)
_C_GG = (512, 768)
_C_GLR = (768, 896)
_C_AQ = (896, 1408)
_C_AK = (1408, 1536)
_C_AV = (1536, 1664)
_C_HY = (1664, 2432)
_NPACK = 2432


def _head_mean_matrix():
    r = lax.broadcasted_iota(jnp.int32, (LANES, LANES), 0) // HEAD_DIM
    c = lax.broadcasted_iota(jnp.int32, (LANES, LANES), 1) // HEAD_DIM
    return jnp.where(r == c, 1.0 / HEAD_DIM, 0.0).astype(BF16)


def _norm_rope(a, gain, cos, sin_signed, gmat, first_half):
    ms = jnp.dot((a * a).astype(BF16), gmat, preferred_element_type=F32)
    xn = a * lax.rsqrt(ms + EPS) * gain
    rot = jnp.where(first_half, pltpu.roll(xn, LANES - ROPE_HALF // 2, 1), pltpu.roll(xn, ROPE_HALF // 2, 1))
    return xn * cos + rot * sin_signed


def _inproj_kernel(x_ref, mod_ref, g_ref, w_ref, cos_ref, sin_ref, qg_ref, kg_ref,
                   gqk_ref, gv_ref, gg_ref, glr_ref, q_ref, k_ref, vt_ref, hy_ref, *, tk):
    x = x_ref[0]
    sh = mod_ref[0, 0:1, :]
    sc = mod_ref[0, 1:2, :]
    h = (_rms(x) * g_ref[...] * (1.0 + sc) + sh).astype(BF16)

    def seg(c):
        return jnp.dot(h, w_ref[:, c[0]:c[1]], preferred_element_type=F32)

    gqk_ref[0] = seg(_C_GQK)
    gv_ref[0] = seg(_C_GV)
    gg_ref[0] = seg(_C_GG)
    glr_ref[0] = seg(_C_GLR)
    hy_ref[0] = seg(_C_HY)
    av = seg(_C_AV)
    for ci in range(av.shape[0] // tk):
        vt_ref[0, ci] = av[ci * tk:(ci + 1) * tk, :].T.astype(BF16)

    gmat = _head_mean_matrix()
    cos = cos_ref[...]
    sin_signed = sin_ref[...]
    lane = lax.broadcasted_iota(jnp.int32, cos.shape, 1)
    first_half = (lane % ROPE_HALF) < (ROPE_HALF // 2)
    aq = seg(_C_AQ)
    scale = HEAD_DIM ** -0.5
    for j in range(ATT_WIDTH // LANES):
        qj = _norm_rope(aq[:, j * LANES:(j + 1) * LANES], qg_ref[...], cos, sin_signed, gmat, first_half)
        q_ref[0, :, j * LANES:(j + 1) * LANES] = (qj * scale).astype(BF16)
    k_ref[0] = _norm_rope(seg(_C_AK), kg_ref[...], cos, sin_signed, gmat, first_half).astype(BF16)


def _inproj_call(x, mod_l, g, w_pack, cos2, sin2, qg2, kg2):
    b, l, d = x.shape
    tm = min(PROJ_ROWS, l)
    tk = min(ATT_TK, l)
    row = lambda bi, i: (bi, i, 0)
    full2 = lambda bi, i: (0, 0)
    rows_out = [
        (SDS((b, l, 256), F32), 256), (SDS((b, l, 256), F32), 256), (SDS((b, l, 256), F32), 256),
        (SDS((b, l, 128), F32), 128), (SDS((b, l, ATT_WIDTH), BF16), ATT_WIDTH),
        (SDS((b, l, ATT_KV), BF16), ATT_KV),
    ]
    out_specs = [pl.BlockSpec((1, tm, w), row) for _, w in rows_out]
    out_shape = [s for s, _ in rows_out]
    out_specs.append(pl.BlockSpec((1, tm // tk, ATT_KV, tk), lambda bi, i: (bi, i, 0, 0)))
    out_shape.append(SDS((b, l // tk, ATT_KV, tk), BF16))
    out_specs.append(pl.BlockSpec((1, tm, 768), row))
    out_shape.append(SDS((b, l, 768), F32))
    return pl.pallas_call(
        functools.partial(_inproj_kernel, tk=tk),
        grid=(b, l // tm),
        in_specs=[
            pl.BlockSpec((1, tm, d), row),
            pl.BlockSpec((1, 6, d), lambda bi, i: (bi, 0, 0)),
            pl.BlockSpec((1, d), full2),
            pl.BlockSpec((d, _NPACK), full2),
            pl.BlockSpec((tm, LANES), lambda bi, i: (i, 0)),
            pl.BlockSpec((tm, LANES), lambda bi, i: (i, 0)),
            pl.BlockSpec((1, LANES), full2),
            pl.BlockSpec((1, LANES), full2),
        ],
        out_specs=out_specs,
        out_shape=out_shape,
        compiler_params=_params("parallel", "parallel"),
        name="in_proj",
    )(x, mod_l, g, w_pack, cos2, sin2, qg2, kg2)


def _attn_kernel(q_ref, k_ref, vt_ref, o_ref, acc_ref, s_ref, p_ref, *, tq):
    nchunk = vt_ref.shape[1]
    tk = vt_ref.shape[3]
    lane = lax.broadcasted_iota(jnp.int32, (tq, LANES), 1)
    lower = lane < HEAD_DIM
    ones_rows = ATT_ONES_ROWS
    vrows = HEAD_DIM + ones_rows
    row = lax.broadcasted_iota(jnp.int32, (vrows, tk), 0)
    q = q_ref[0].astype(F32)
    streams = []
    qts = []
    for j in range(ATT_KV_HEADS):
        keep = lower if j == 0 else jnp.logical_not(lower)
        for hp in range(ATT_GROUP // 2):
            cb = j * (ATT_GROUP // 2) + hp
            qc = q[:, cb * LANES:(cb + 1) * LANES]
            qsw = pltpu.roll(qc, HEAD_DIM, 1)
            first, second = (qc, qsw) if j == 0 else (qsw, qc)
            cols = [jnp.where(keep, first, 0.0).T, jnp.where(keep, second, 0.0).T]
            qts.append(jnp.concatenate(cols, axis=1).astype(BF16))
            streams.append((j, cb))
    qt_all = jnp.concatenate(qts, axis=1)
    half = ATT_GROUP * tq
    acc_ref[...] = jnp.zeros(acc_ref.shape, F32)
    p_ref[1] = jnp.zeros(p_ref.shape[1:], BF16)
    ones = jnp.ones((vrows, tk), BF16)

    def scores(c):
        kc = k_ref[0, pl.ds(pl.multiple_of(c * tk, tk), tk), :]
        return jnp.dot(kc, qt_all, preferred_element_type=F32)

    def add_values(c, slot, alpha):
        vt = vt_ref[0, c]
        for j in range(ATT_KV_HEADS):
            r0 = j * (LANES - vrows)
            vm = jnp.where((row < HEAD_DIM) if j == 0 else (row >= ones_rows), vt[r0:r0 + vrows, :], ones)
            pj = p_ref[slot, :, j * half:(j + 1) * half]
            acc_ref[j] = acc_ref[j] * alpha[:, j * half:(j + 1) * half] + jnp.dot(vm, pj, preferred_element_type=F32)

    s_ref[0] = scores(0)

    def step(c, slot, carry):
        m_prev, alpha_prev = carry
        s_ref[1 - slot] = scores(jnp.minimum(c + 1, nchunk - 1))
        add_values(jnp.maximum(c - 1, 0), 1 - slot, alpha_prev)
        st = s_ref[slot]
        m_new = jnp.maximum(m_prev, jnp.max(st, axis=0, keepdims=True))
        p_ref[slot] = jnp.exp((st - m_new).astype(BF16))
        return m_new, jnp.exp(m_prev - m_new)

    def body(i, carry):
        return step(2 * i + 1, 1, step(2 * i, 0, carry))

    assert nchunk % 2 == 0
    init = (jnp.full((1, 2 * half), -jnp.inf, F32), jnp.ones((1, 2 * half), F32))
    _, alpha = lax.fori_loop(0, nchunk // 2, body, init)
    add_values(nchunk - 1, 1, alpha)

    for j, cb in streams:
        hp = cb % (ATT_GROUP // 2)
        acc = acc_ref[j][:, hp * 2 * tq:(hp + 1) * 2 * tq]
        d0 = j * ones_rows
        denom = acc[(1 - j) * HEAD_DIM:(1 - j) * HEAD_DIM + 1, :]
        data = acc[d0:d0 + HEAD_DIM, :] / denom
        blk = jnp.concatenate([data[:, :tq], data[:, tq:]], axis=0)
        o_ref[0, :, cb * LANES:(cb + 1) * LANES] = blk.T.astype(BF16)


def _attn_call(q, k, vt):
    b, l, _ = q.shape
    tq = min(ATT_TQ, l)
    nchunk, tk = vt.shape[1], vt.shape[3]
    nq = ATT_HEADS * tq
    return pl.pallas_call(
        functools.partial(_attn_kernel, tq=tq),
        grid=(b, l // tq),
        in_specs=[
            pl.BlockSpec((1, tq, ATT_WIDTH), lambda bi, i: (bi, i, 0)),
            pl.BlockSpec((1, l, ATT_KV), lambda bi, i: (bi, 0, 0)),
            pl.BlockSpec((1, nchunk, ATT_KV, tk), lambda bi, i: (bi, 0, 0, 0)),
        ],
        out_specs=pl.BlockSpec((1, tq, ATT_WIDTH), lambda bi, i: (bi, i, 0)),
        out_shape=SDS((b, l, ATT_WIDTH), BF16),
        scratch_shapes=[pltpu.VMEM((ATT_KV_HEADS, HEAD_DIM + ATT_ONES_ROWS, nq // ATT_KV_HEADS), F32),
                        pltpu.VMEM((2, tk, nq), F32), pltpu.VMEM((2, tk, nq), BF16)],
        compiler_params=_params("parallel", "parallel"),
        name="gqa_attention",
    )(q, k, vt)


def _log_sigmoid(x):
    return jnp.minimum(x, 0.0) - jnp.log(1.0 + jnp.exp(-jnp.abs(x)))


def _gla_kernel(qk_ref, v_ref, gg_ref, lr_ref, wdec_ref, bdec_ref, ng_ref, o_ref,
                qs_ref, ks_ref, a_ref, od_ref, *, rt):
    seq = qk_ref.shape[1]
    cb = GLA_BLOCK
    sb = GLA_STATE_BLOCK
    rowi = lax.broadcasted_iota(jnp.int32, (rt, GLA_QK), 0)
    rowmod = rowi % cb
    hd = lax.broadcasted_iota(jnp.int32, (GLA_QK, GLA_WIDTH), 0) // GLA_DK
    hc = lax.broadcasted_iota(jnp.int32, (GLA_QK, GLA_WIDTH), 1) // GLA_DV
    expand = jnp.where(hd == hc, 1.0, 0.0).astype(BF16)
    qk_head = lax.broadcasted_iota(jnp.int32, (rt, GLA_QK), 1) // GLA_DK
    v_head = lax.broadcasted_iota(jnp.int32, (rt, GLA_WIDTH), 1) // GLA_DV
    pair_i = lax.broadcasted_iota(jnp.int32, (rt, rt), 0)
    pair_j = lax.broadcasted_iota(jnp.int32, (rt, rt), 1)
    scale = GLA_DK ** -0.5

    def prep(t, _):
        r0 = pl.multiple_of(t * rt, rt)
        rows = pl.ds(r0, rt)
        q = qk_ref[0, rows, 0:GLA_QK] * scale
        k = qk_ref[0, rows, GLA_QK:2 * GLA_QK]
        v = v_ref[0, rows, :]
        lr = lr_ref[0, rows, :]
        od = jnp.zeros((rt, GLA_WIDTH), F32)
        half_sums = []
        for direction in range(2):
            la = _log_sigmoid(jnp.dot(lr, wdec_ref[direction], precision=HI, preferred_element_type=F32)
                              + bdec_ref[direction]) * (1.0 / GLA_GATE_NORM)
            pre = la
            suf = la
            s = 1
            while s < cb:
                pre = pre + jnp.where(rowmod >= s, pltpu.roll(pre, s, 0), 0.0)
                suf = suf + jnp.where(rowmod < cb - s, pltpu.roll(suf, rt - s, 0), 0.0)
                s *= 2
            tot = pre + suf - la
            sums = {}
            size = cb
            while size < sb:
                sums[size] = (pre, suf, la)
                upper = (rowi % (2 * size)) >= size
                t_below = pltpu.roll(tot, size, 0)
                t_above = pltpu.roll(tot, rt - size, 0)
                pre = pre + jnp.where(upper, t_below, 0.0)
                suf = suf + jnp.where(upper, 0.0, t_above)
                tot = tot + jnp.where(upper, t_below, t_above)
                size *= 2
            half_sums.append(sums)
            cum = pre if direction == 0 else suf
            rem = (suf if direction == 0 else pre) - la
            qs_ref[direction, rows, :] = q * jnp.exp(cum)
            ks_ref[direction, rows, :] = k * jnp.exp(rem)
            a_ref[direction, rows, :] = jnp.exp(tot)
            for delta in range(cb):
                if delta == 0:
                    a_mat = q * k
                    vj = v
                else:
                    if direction == 0:
                        sh = delta
                        valid = rowmod >= delta
                    else:
                        sh = rt - delta
                        valid = rowmod < cb - delta
                    kj = pltpu.roll(k, sh, 0)
                    cj = pltpu.roll(cum, sh, 0)
                    vj = pltpu.roll(v, sh, 0)
                    a_mat = jnp.where(valid, q * kj * jnp.exp(cum - cj), 0.0)
                se = jnp.dot(a_mat.astype(BF16), expand, preferred_element_type=F32)
                od = od + se * vj
        size = cb
        while size < sb:
            upper = (rowi % (2 * size)) >= size
            pre_f, suf_f, la_f = half_sums[0][size]
            pre_b, suf_b, la_b = half_sums[1][size]
            qf = (q * jnp.exp(jnp.where(upper, pre_f, suf_b))).astype(BF16)
            kf = (k * jnp.exp(jnp.where(upper, pre_b - la_b, suf_f - la_f))).astype(BF16)
            same_block = (pair_i // (2 * size)) == (pair_j // (2 * size))
            cross = ((pair_i % (2 * size)) >= size) != ((pair_j % (2 * size)) >= size)
            keep = jnp.logical_and(same_block, cross)
            for h in range(GLA_HEADS):
                qh = jnp.where(qk_head == h, qf, jnp.zeros_like(qf))
                sc = lax.dot_general(qh, kf, (((1,), (1,)), ((), ())), preferred_element_type=F32)
                vh = jnp.where(v_head == h, v, 0.0).astype(BF16)
                od = od + jnp.dot(jnp.where(keep, sc, 0.0).astype(BF16), vh, preferred_element_type=F32)
            size *= 2
        od_ref[rows, :] = od
        return 0

    lax.fori_loop(0, seq // rt, prep, 0)

    sr = lax.broadcasted_iota(jnp.int32, (GLA_WIDTH, GLA_QK), 0) // GLA_DV
    scol = lax.broadcasted_iota(jnp.int32, (GLA_WIDTH, GLA_QK), 1) // GLA_DK
    same_head = sr == scol
    nblk = seq // sb
    unroll = min(GLA_STEP_UNROLL, nblk)
    span = unroll * sb

    def step(i, states):
        states = list(states)
        for direction in range(2):
            base = i * span if direction == 0 else seq - (i + 1) * span
            rows = pl.ds(pl.multiple_of(base, span), span)
            qb = qs_ref[direction, rows, :].astype(BF16)
            kb = ks_ref[direction, rows, :].astype(BF16)
            vb = v_ref[0, rows, :].astype(BF16)
            ab = a_ref[direction, rows, :]
            st = states[direction]
            outs = [None] * unroll
            for u in (range(unroll) if direction == 0 else range(unroll - 1, -1, -1)):
                sl = slice(u * sb, (u + 1) * sb)
                outs[u] = lax.dot_general(qb[sl], st.astype(BF16), (((1,), (1,)), ((), ())),
                                          preferred_element_type=F32)
                kv = lax.dot_general(vb[sl], kb[sl], (((0,), (0,)), ((), ())), preferred_element_type=F32)
                st = st * ab[u * sb:u * sb + 1, :] + jnp.where(same_head, kv, 0.0)
            od_ref[rows, :] += jnp.concatenate(outs, axis=0)
            states[direction] = st
        return tuple(states)

    zero_state = jnp.zeros((GLA_WIDTH, GLA_QK), F32)
    lax.fori_loop(0, nblk // unroll, step, (zero_state, zero_state))

    gr = lax.broadcasted_iota(jnp.int32, (GLA_WIDTH, GLA_WIDTH), 0) // GLA_DV
    gc = lax.broadcasted_iota(jnp.int32, (GLA_WIDTH, GLA_WIDTH), 1) // GLA_DV
    gmat = jnp.where(gr == gc, 1.0 / GLA_DV, 0.0).astype(BF16)

    def fin(t, _):
        rows = pl.ds(pl.multiple_of(t * rt, rt), rt)
        o = od_ref[rows, :]
        ms = jnp.dot((o * o).astype(BF16), gmat, preferred_element_type=F32)
        o = o * lax.rsqrt(ms + EPS) * ng_ref[...]
        o_ref[0, rows, :] = (o * _silu(gg_ref[0, rows, :])).astype(BF16)
        return 0

    lax.fori_loop(0, seq // rt, fin, 0)


def _gla_call(gqk, gv, gg, glr, wdec_pad, bdec, ng4):
    b, l, _ = gqk.shape
    rt = min(GLA_ROWS, l)
    once = dict(pipeline_mode=pl.Buffered(1))
    bmap = lambda bi: (bi, 0, 0)
    return pl.pallas_call(
        functools.partial(_gla_kernel, rt=rt),
        grid=(b,),
        in_specs=[
            pl.BlockSpec((1, l, 2 * GLA_QK), bmap, **once),
            pl.BlockSpec((1, l, GLA_WIDTH), bmap, **once),
            pl.BlockSpec((1, l, GLA_WIDTH), bmap, **once),
            pl.BlockSpec((1, l, LANES), bmap, **once),
            pl.BlockSpec((2, LANES, GLA_QK), lambda bi: (0, 0, 0)),
            pl.BlockSpec((2, 1, GLA_QK), lambda bi: (0, 0, 0)),
            pl.BlockSpec((1, GLA_WIDTH), lambda bi: (0, 0)),
        ],
        out_specs=pl.BlockSpec((1, l, GLA_WIDTH), bmap),
        out_shape=SDS((b, l, GLA_WIDTH), BF16),
        scratch_shapes=[
            pltpu.VMEM((2, l, GLA_QK), F32),
            pltpu.VMEM((2, l, GLA_QK), F32),
            pltpu.VMEM((2, l, GLA_QK), F32),
            pltpu.VMEM((l, GLA_WIDTH), F32),
        ],
        compiler_params=_params("parallel"),
        name="gla",
    )(gqk, gv, gg, glr, wdec_pad, bdec, ng4)


def _hy_pre_kernel(x0_ref, x1_ref, v_ref, w0_ref, w1_ref, wv_ref, b0_ref, b1_ref, bv_ref, z_ref, x0o_ref, *, rt):
    seq = x0_ref.shape[1]
    rowid = lax.broadcasted_iota(jnp.int32, (rt, LANES), 0)

    def conv(ref, w_ref, b_ref, r0):
        cur = ref[0, pl.ds(r0, rt), :]
        g = SUBLANES
        prev = ref[0, pl.ds(pl.multiple_of(jnp.maximum(r0 - g, 0), g), g), :][g - 1:g, :]
        nxt = ref[0, pl.ds(pl.multiple_of(jnp.minimum(r0 + rt, seq - g), g), g), :][0:1, :]
        prev = jnp.where(r0 > 0, prev, 0.0)
        nxt = jnp.where(r0 + rt < seq, nxt, 0.0)
        up = jnp.where(rowid == 0, prev, pltpu.roll(cur, 1, 0))
        down = jnp.where(rowid == rt - 1, nxt, pltpu.roll(cur, rt - 1, 0))
        return up * w_ref[0:1, :] + cur * w_ref[1:2, :] + down * w_ref[2:3, :] + b_ref[...]

    def body(t, _):
        r0 = pl.multiple_of(t * rt, rt)
        rows = pl.ds(r0, rt)
        x0o_ref[0, rows, :] = conv(x0_ref, w0_ref, b0_ref, r0)
        z_ref[0, rows, :] = conv(v_ref, wv_ref, bv_ref, r0) * conv(x1_ref, w1_ref, b1_ref, r0)
        return 0

    lax.fori_loop(0, seq // rt, body, 0)


def _hy_pre_call(hy, conv_w, conv_b):
    b, l, w3 = hy.shape
    hw = w3 // 3
    nj = hw // LANES
    rt = min(HY_PRE_ROWS, l)
    xs = lambda off: pl.BlockSpec((1, l, LANES), lambda bi, j: (bi, 0, off * nj + j))
    ws = lambda off: pl.BlockSpec((3, LANES), lambda bi, j: (0, off * nj + j))
    bs = lambda off: pl.BlockSpec((1, LANES), lambda bi, j: (0, off * nj + j))
    out = pl.BlockSpec((1, l, LANES), lambda bi, j: (bi, 0, j))
    return pl.pallas_call(
        functools.partial(_hy_pre_kernel, rt=rt),
        grid=(b, nj),
        in_specs=[xs(0), xs(1), xs(2), ws(0), ws(1), ws(2), bs(0), bs(1), bs(2)],
        out_specs=[out, out],
        out_shape=[SDS((b, l, hw), F32), SDS((b, l, hw), F32)],
        compiler_params=_params("parallel", "parallel"),
        name="hyena_short_conv",
    )(hy, hy, hy, conv_w, conv_w, conv_w, conv_b, conv_b, conv_b)


def _hy_filter_kernel(z_ref, w1_ref, b1_ref, w2_ref, b2_ref, w3_ref, fr_ref, dl_ref, o_ref, *, seq, tr):
    half = tr // 2
    zs = (z_ref[0:half, :], z_ref[half:tr, :])
    fr = fr_ref[...]
    pre = (jnp.dot(zs[0], w1_ref[0], precision=HI, preferred_element_type=F32)
           + jnp.dot(zs[1], w1_ref[1], precision=HI, preferred_element_type=F32))
    h = jnp.sin(fr * (pre + b1_ref[...]))
    for i in range(w2_ref.shape[0]):
        h = jnp.sin(fr * (jnp.dot(h, w2_ref[i], precision=HI, preferred_element_type=F32) + b2_ref[i]))
    c = dl_ref.shape[1]
    for part in range(2):
        hw = jnp.dot(h, w3_ref[part], precision=HI, preferred_element_type=F32)
        window = jnp.exp(-zs[part][:, 0:1] * dl_ref[...])
        m = pl.program_id(0) * tr + part * half + lax.broadcasted_iota(jnp.int32, (half, c), 0)
        tap = jnp.where(m >= seq, hw[:, :c], hw[:, c:]) * window
        o_ref[part * half:(part + 1) * half, :] = jnp.where(m == 0, 0.0, tap)


def _hy_filter_call(feat, w1p, b1, w2, b2, w3, freq, deltas, seq):
    n2, _ = feat.shape
    od = w1p.shape[1]
    c2 = w3.shape[1]
    tr = min(HY_FILTER_ROWS, n2)
    zero = lambda a: jnp.zeros_like(a)
    w1_2 = jnp.stack([jnp.concatenate([w1p, zero(w1p)], 1), jnp.concatenate([zero(w1p), w1p], 1)])
    w2_2 = jnp.concatenate([jnp.concatenate([w2, zero(w2)], 2), jnp.concatenate([zero(w2), w2], 2)], 1)
    w3_2 = jnp.stack([jnp.concatenate([w3, zero(w3)], 0), jnp.concatenate([zero(w3), w3], 0)])
    twice = lambda a: jnp.concatenate([a, a], -1)
    full = lambda *shape: pl.BlockSpec(shape, lambda i: (0,) * len(shape))
    return pl.pallas_call(
        functools.partial(_hy_filter_kernel, seq=seq, tr=tr),
        grid=(n2 // tr,),
        in_specs=[
            pl.BlockSpec((tr, LANES), lambda i: (i, 0)),
            full(2, LANES, 2 * od), full(1, 2 * od), full(w2.shape[0], 2 * od, 2 * od), full(w2.shape[0], 1, 2 * od),
            full(2, 2 * od, c2), full(1, 2 * od), full(1, c2 // 2),
        ],
        out_specs=pl.BlockSpec((tr, c2 // 2), lambda i: (i, 0)),
        out_shape=SDS((n2, c2 // 2), F32),
        compiler_params=_params("parallel"),
        name="hyena_filter",
    )(feat, w1_2, twice(b1), w2_2, twice(b2), w3_2, twice(freq), deltas)


def _hy_conv_kernel(z_ref, f_ref, y_ref, acc_ref, *, nb, bp):
    cg = z_ref.shape[0]
    p = HY_BLOCK
    ii = lax.broadcasted_iota(jnp.int32, (p, p), 0)
    jj = lax.broadcasted_iota(jnp.int32, (p, p), 1)
    upper = jj >= ii

    rows = nb * bp
    sub = SUBLANES
    shifts = sorted({(bp * d) % sub for d in range(-(nb - 1), nb)})

    def chan(ci, _):
        pad = jnp.zeros((sub, p), F32)
        zfull = jnp.concatenate([pad, z_ref[ci], pad], axis=0)
        zs = {m: (zfull if m == 0 else pltpu.roll(zfull, m, 0)) for m in shifts}
        acc_ref[...] = jnp.zeros(acc_ref.shape, F32)

        def skew(k):
            taps = jnp.broadcast_to(f_ref[ci, k:k + 1, :], (p, p))
            return pltpu.roll(taps, 0, 1, stride=1, stride_axis=0).astype(BF16)

        hi = skew(0)
        for k in range(2 * nb - 1):
            delta = k - (nb - 1)
            lo, hi = hi, skew(k + 1)
            w = jnp.where(upper, hi, lo)
            s = bp * delta
            o0 = max(0, s) // sub * sub
            o1 = -(-min(rows, rows + s) // sub) * sub
            m = s % sub
            src = o0 - s + sub + m
            lhs = zs[m][src:src + (o1 - o0), :].astype(BF16)
            acc_ref[o0:o1, :] += jnp.dot(lhs, w, preferred_element_type=F32)
        y_ref[ci] = acc_ref[...]
        return 0

    lax.fori_loop(0, cg, chan, 0)


def _hy_conv_call(zt, ft, nb, bp):
    c, rows, p = zt.shape
    cg = min(HY_CONV_CHANNELS, c)
    return pl.pallas_call(
        functools.partial(_hy_conv_kernel, nb=nb, bp=bp),
        grid=(c // cg,),
        in_specs=[
            pl.BlockSpec((cg, rows, p), lambda g: (g, 0, 0)),
            pl.BlockSpec((cg, 2 * nb, p), lambda g: (g, 0, 0)),
        ],
        out_specs=pl.BlockSpec((cg, rows, p), lambda g: (g, 0, 0)),
        out_shape=SDS((c, rows, p), F32),
        scratch_shapes=[pltpu.VMEM((rows, p), F32)],
        compiler_params=_params("parallel"),
        name="hyena_long_conv",
    )(zt, ft)


def _outproj_kernel(x_ref, ogla_ref, oatt_ref, yc_ref, z_ref, x0_ref, skip_ref, w_ref, mod_ref,
                    gpost_ref, gpre_ref, rw_ref, xo_ref, h2_ref, aff_ref, *, n_exp, sub):
    g0 = GLA_WIDTH
    g1 = GLA_WIDTH + ATT_WIDTH
    g_m = mod_ref[0, 2:3, :]
    sh_f = mod_ref[0, 3:4, :]
    sc_f = mod_ref[0, 4:5, :]
    for s in range(x_ref.shape[1] // sub):
        rows = slice(s * sub, (s + 1) * sub)
        ohy = (x0_ref[0, rows, :] * (yc_ref[0, rows, :] + z_ref[0, rows, :] * skip_ref[...])).astype(BF16)
        y = jnp.dot(ogla_ref[0, rows, :], w_ref[0:g0, :], preferred_element_type=F32)
        y = y + jnp.dot(oatt_ref[0, rows, :], w_ref[g0:g1, :], preferred_element_type=F32)
        y = y + jnp.dot(ohy, w_ref[g1:, :], preferred_element_type=F32)
        x1 = x_ref[0, rows, :] + g_m * (_rms(y) * gpost_ref[...])
        xo_ref[0, rows, :] = x1
        h2 = _rms(x1) * gpre_ref[...] * (1.0 + sc_f) + sh_f
        h2_ref[0, :, rows] = h2.T.astype(BF16)
        h_hi = h2.astype(BF16)
        h_lo = (h2 - h_hi.astype(F32)).astype(BF16)
        logits = (jnp.dot(h_hi, rw_ref[0], preferred_element_type=F32)
                  + jnp.dot(h_lo, rw_ref[0], preferred_element_type=F32)
                  + jnp.dot(h_hi, rw_ref[1], preferred_element_type=F32))
        lane = lax.broadcasted_iota(jnp.int32, logits.shape, 1)
        logits = jnp.where(lane < n_exp, logits, -jnp.inf)
        e = jnp.exp(logits - jnp.max(logits, axis=-1, keepdims=True))
        aff_ref[0, rows, :] = e / jnp.sum(e, axis=-1, keepdims=True)


def _outproj_call(x, ogla, oatt, yc, z, x0, skip, w_out, mod_l, gpost, gpre, rw_pad, n_exp):
    b, l, d = x.shape
    tm = min(PROJ_ROWS, l)
    row = lambda bi, i: (bi, i, 0)
    full2 = lambda bi, i: (0, 0)
    hw = yc.shape[-1]
    return pl.pallas_call(
        functools.partial(_outproj_kernel, n_exp=n_exp, sub=min(OUT_SUB_ROWS, tm)),
        grid=(b, l // tm),
        in_specs=[
            pl.BlockSpec((1, tm, d), row),
            pl.BlockSpec((1, tm, GLA_WIDTH), row),
            pl.BlockSpec((1, tm, ATT_WIDTH), row),
            pl.BlockSpec((1, tm, hw), row),
            pl.BlockSpec((1, tm, hw), row),
            pl.BlockSpec((1, tm, hw), row),
            pl.BlockSpec((1, hw), full2),
            pl.BlockSpec((d, d), full2),
            pl.BlockSpec((1, 6, d), lambda bi, i: (bi, 0, 0)),
            pl.BlockSpec((1, d), full2),
            pl.BlockSpec((1, d), full2),
            pl.BlockSpec((2, d, LANES), lambda bi, i: (0, 0, 0)),
        ],
        out_specs=[pl.BlockSpec((1, tm, d), row), pl.BlockSpec((1, d, tm), lambda bi, i: (bi, 0, i)),
                   pl.BlockSpec((1, tm, LANES), row)],
        out_shape=[SDS((b, l, d), F32), SDS((b, d, l), BF16), SDS((b, l, LANES), F32)],
        compiler_params=_params("parallel", "parallel"),
        name="out_proj_router",
    )(x, ogla, oatt, yc, z, x0, skip, w_out, mod_l, gpost, gpre, rw_pad)


def _route_kernel(aff_ref, pos_ref, cum_ref, *, cap):
    aff = aff_ref[0]
    n_exp, t = aff.shape

    def bisect(_, bounds):
        lo, hi = bounds
        mid = 0.5 * (lo + hi)
        ok = jnp.sum(jnp.where(aff >= mid, 1.0, 0.0), axis=1, keepdims=True) >= cap
        return jnp.where(ok, mid, lo), jnp.where(ok, hi, mid)

    lo, hi = lax.fori_loop(0, ROUTE_BISECTIONS, bisect,
                           (jnp.zeros((n_exp, 1), F32), jnp.full((n_exp, 1), 2.0, F32)))
    need = cap - jnp.sum(jnp.where(aff >= hi, 1.0, 0.0), axis=1, keepdims=True)
    r = lax.broadcasted_iota(jnp.int32, (LANES, LANES), 0)
    c = lax.broadcasted_iota(jnp.int32, (LANES, LANES), 1)
    tri = jnp.where(r <= c, 1.0, 0.0).astype(BF16)
    eq_run = jnp.zeros((n_exp, 1), F32)
    sel_run = jnp.zeros((n_exp, 1), F32)
    runs = []
    for j in range(t // LANES):
        sl = slice(j * LANES, (j + 1) * LANES)
        aj = aff[:, sl]
        eq_j = jnp.where(aj >= hi, 0.0, jnp.where(aj >= lo, 1.0, 0.0))
        eq_cum = jnp.dot(eq_j.astype(BF16), tri, preferred_element_type=F32) + eq_run
        sel_j = jnp.where(aj >= hi, 1.0, jnp.where(eq_cum <= need, eq_j, 0.0))
        sel_cum = jnp.dot(sel_j.astype(BF16), tri, preferred_element_type=F32) + sel_run
        pos_ref[0, :, sl] = jnp.where(sel_j > 0.0, sel_cum - 1.0, -1.0).astype(jnp.int32)
        eq_run = eq_cum[:, LANES - 1:LANES]
        sel_run = sel_cum[:, LANES - 1:LANES]
        runs.append(sel_run)
    cum_ref[0] = jnp.concatenate(runs, axis=1).astype(jnp.int32)


def _route_call(aff_t, cap):
    b, n_exp, t = aff_t.shape
    rows = b * n_exp
    pos, cum = pl.pallas_call(
        functools.partial(_route_kernel, cap=cap),
        grid=(1,),
        in_specs=[pl.BlockSpec((1, rows, t), lambda i: (0, 0, 0))],
        out_specs=[pl.BlockSpec((1, rows, t), lambda i: (0, 0, 0)),
                   pl.BlockSpec((1, rows, t // LANES), lambda i: (0, 0, 0))],
        out_shape=[SDS((1, rows, t), jnp.int32), SDS((1, rows, t // LANES), jnp.int32)],
        compiler_params=_params("arbitrary"),
        name="ec_route",
    )(aff_t.reshape(1, rows, t))
    return pos.reshape(rows, 1, t), cum.reshape(rows, t // LANES)


def _chunk_ranges(cum, cap, tc):
    per = tc // LANES
    cum_c = cum[:, per - 1::per]
    first = jnp.arange(0, cap, EC_SLOT_BLOCK, dtype=jnp.int32)
    chunk_of = lambda s: jnp.sum(cum_c[:, None, :] <= s[None, :, None], axis=-1).astype(jnp.int32)
    return chunk_of(first).reshape(-1), chunk_of(first + EC_SLOT_BLOCK - 1).reshape(-1)


def _one_hot(pos_ref, off, tc, first_slot):
    slot = first_slot + lax.broadcasted_iota(jnp.int32, (EC_SLOT_BLOCK, tc), 0)
    return jnp.where(slot == pos_ref[0, :, pl.ds(off, tc)], 1.0, 0.0).astype(BF16)


def _dispatch_kernel(lo_ref, hi_ref, pos_ref, aff_ref, ht_ref, xgt_ref, gc_ref, acc_ref, gacc_ref, *, tc, no_slot):
    nr = pl.num_programs(2)
    r = pl.program_id(2)
    idx = (pl.program_id(0) * pl.num_programs(1) + pl.program_id(1)) * nr + r
    acc_ref[...] = jnp.zeros(acc_ref.shape, F32)
    gacc_ref[...] = jnp.zeros(gacc_ref.shape, F32)
    nt = (((1,), (1,)), ((), ()))

    lo = lo_ref[idx]
    hi = hi_ref[idx]

    def body(i, _):
        c0 = lo + 2 * i
        chunks = ((c0, r * EC_SLOT_BLOCK),
                  (jnp.minimum(c0 + 1, hi), jnp.where(c0 + 1 <= hi, r * EC_SLOT_BLOCK, no_slot)))
        part = gpart = None
        for c, first_slot in chunks:
            off = pl.multiple_of(c * tc, tc)
            sel = _one_hot(pos_ref, off, tc, first_slot)
            d = lax.dot_general(ht_ref[0, :, pl.ds(off, tc)], sel, nt, preferred_element_type=F32)
            g = aff_ref[0, :, pl.ds(off, tc)]
            g1 = g.astype(BF16).astype(F32)
            g2 = (g - g1).astype(BF16).astype(F32)
            g3 = g - g1 - g2
            prow = lax.broadcasted_iota(jnp.int32, (gacc_ref.shape[0], tc), 0)
            pieces = jnp.where(prow == 0, g1, jnp.where(prow == 1, g2, jnp.where(prow == 2, g3, 0.0))).astype(BF16)
            gd = lax.dot_general(pieces, sel, nt, preferred_element_type=F32)
            part = d if part is None else part + d
            gpart = gd if gpart is None else gpart + gd
        acc_ref[...] += part
        gacc_ref[...] += gpart
        return 0

    lax.fori_loop(0, (hi - lo + 2) // 2, body, 0)
    xgt_ref[0] = acc_ref[...].astype(BF16)
    gc_ref[0] = jnp.sum(gacc_ref[...], axis=0, keepdims=True)


def _dispatch_call(lo, hi, pos_rows, aff_rows, h2t, n_exp, cap):
    b, d, t = h2t.shape
    tc = min(EC_TOKEN_CHUNK, t)
    nr = cap // EC_SLOT_BLOCK
    row = lambda bi, e, r, lo_r, hi_r: (bi * n_exp + e, 0, 0)
    out = lambda bi, e, r, lo_r, hi_r: (e, 0, bi * nr + r)
    return pl.pallas_call(
        functools.partial(_dispatch_kernel, tc=tc, no_slot=cap),
        grid_spec=pltpu.PrefetchScalarGridSpec(
            num_scalar_prefetch=2,
            grid=(b, n_exp, nr),
            in_specs=[
                pl.BlockSpec((1, 1, t), row),
                pl.BlockSpec((1, 1, t), row),
                pl.BlockSpec((1, d, t), lambda bi, e, r, lo_r, hi_r: (bi, 0, 0)),
            ],
            out_specs=[pl.BlockSpec((1, d, EC_SLOT_BLOCK), out), pl.BlockSpec((1, 1, EC_SLOT_BLOCK), out)],
            scratch_shapes=[pltpu.VMEM((d, EC_SLOT_BLOCK), F32), pltpu.VMEM((2 * SUBLANES, EC_SLOT_BLOCK), F32)],
        ),
        out_shape=[SDS((n_exp, d, b * cap), BF16), SDS((n_exp, 1, b * cap), F32)],
        compiler_params=_params("parallel", "arbitrary", "arbitrary"),
        name="ec_dispatch",
    )(lo, hi, pos_rows, aff_rows, h2t)


def _ffn_kernel(xgt_ref, gc_ref, wg_ref, wu_ref, wd_ref, yt_ref, xg_ref, acc_ref, *, mt):
    f = pl.program_id(1)
    m = xg_ref.shape[0]

    @pl.when(f == 0)
    def _():
        acc_ref[...] = jnp.zeros(acc_ref.shape, F32)
        for i in range(m // mt):
            xg_ref[i * mt:(i + 1) * mt, :] = xgt_ref[0, :, i * mt:(i + 1) * mt].T

    wg = wg_ref[0, 0].astype(BF16)
    wu = wu_ref[0, 0].astype(BF16)
    wd = wd_ref[0, 0].astype(BF16)
    for i in range(m // mt):
        rows = slice(i * mt, (i + 1) * mt)
        xb = xg_ref[rows, :]
        a = jnp.dot(xb, wg, preferred_element_type=F32)
        u = jnp.dot(xb, wu, preferred_element_type=F32)
        acc_ref[rows, :] += jnp.dot((_silu(a) * u).astype(BF16), wd, preferred_element_type=F32)

    @pl.when(f == pl.num_programs(1) - 1)
    def _():
        for i in range(m // mt):
            cols = slice(i * mt, (i + 1) * mt)
            yt_ref[0, :, cols] = (acc_ref[cols, :].T * gc_ref[0, :, cols]).astype(BF16)


def _ffn_call(xgt, gc, w_gate, w_up, w_down, layer):
    n_exp, d, m = xgt.shape
    ff = w_gate.shape[3]
    tf = min(FFN_COLS, ff)
    mt = min(FFN_ROWS, m)
    return pl.pallas_call(
        functools.partial(_ffn_kernel, mt=mt),
        grid=(n_exp, ff // tf),
        in_specs=[
            pl.BlockSpec((1, d, m), lambda e, f: (e, 0, 0)),
            pl.BlockSpec((1, 1, m), lambda e, f: (e, 0, 0)),
            pl.BlockSpec((1, 1, d, tf), lambda e, f: (layer, e, 0, f)),
            pl.BlockSpec((1, 1, d, tf), lambda e, f: (layer, e, 0, f)),
            pl.BlockSpec((1, 1, tf, d), lambda e, f: (layer, e, f, 0)),
        ],
        out_specs=pl.BlockSpec((1, d, m), lambda e, f: (e, 0, 0)),
        out_shape=SDS((n_exp, d, m), BF16),
        scratch_shapes=[pltpu.VMEM((m, d), BF16), pltpu.VMEM((m, d), F32)],
        compiler_params=_params("parallel", "arbitrary"),
        name="expert_ffn",
    )(xgt, gc, w_gate, w_up, w_down)


def _combine_kernel(lo_ref, hi_ref, pos_ref, yt_ref, o_ref, *, tc, no_slot):
    ne = pl.num_programs(1)
    nr = pl.num_programs(2)
    e = pl.program_id(1)
    r = pl.program_id(2)
    idx = (pl.program_id(0) * ne + e) * nr + r

    @pl.when(jnp.logical_and(e == 0, r == 0))
    def _():
        o_ref[...] = jnp.zeros(o_ref.shape, F32)

    yb = yt_ref[0]
    lo = lo_ref[idx]
    hi = hi_ref[idx]

    def body(i, _):
        c0 = lo + 2 * i
        chunks = ((c0, r * EC_SLOT_BLOCK),
                  (jnp.minimum(c0 + 1, hi), jnp.where(c0 + 1 <= hi, r * EC_SLOT_BLOCK, no_slot)))
        adds = []
        for c, first_slot in chunks:
            off = pl.multiple_of(c * tc, tc)
            adds.append((off, jnp.dot(yb, _one_hot(pos_ref, off, tc, first_slot), preferred_element_type=F32)))
        for off, contrib in adds:
            o_ref[0, :, pl.ds(off, tc)] += contrib
        return 0

    lax.fori_loop(0, (hi - lo + 2) // 2, body, 0)


def _combine_call(lo, hi, pos_rows, yt, b, t, cap):
    n_exp, d, _ = yt.shape
    tc = min(EC_TOKEN_CHUNK, t)
    nr = cap // EC_SLOT_BLOCK
    return pl.pallas_call(
        functools.partial(_combine_kernel, tc=tc, no_slot=cap),
        grid_spec=pltpu.PrefetchScalarGridSpec(
            num_scalar_prefetch=2,
            grid=(b, n_exp, nr),
            in_specs=[
                pl.BlockSpec((1, 1, t), lambda bi, e, r, lo_r, hi_r: (bi * n_exp + e, 0, 0)),
                pl.BlockSpec((1, d, EC_SLOT_BLOCK), lambda bi, e, r, lo_r, hi_r: (e, 0, bi * nr + r)),
            ],
            out_specs=pl.BlockSpec((1, d, t), lambda bi, e, r, lo_r, hi_r: (bi, 0, 0)),
        ),
        out_shape=SDS((b, d, t), F32),
        compiler_params=_params("parallel", "arbitrary", "arbitrary"),
        name="ec_combine",
    )(lo, hi, pos_rows, yt)


def _ffn_residual_kernel(acc_ref, x_ref, mod_ref, g_ref, o_ref):
    y = acc_ref[0].T
    g_f = mod_ref[0, 5:6, :]
    o_ref[0] = x_ref[0] + g_f * (_rms(y) * g_ref[...])


def _ffn_residual_call(acc_t, x, mod_l, gpost):
    b, t, d = x.shape
    tt = min(RESID_ROWS, t)
    return pl.pallas_call(
        _ffn_residual_kernel,
        grid=(b, t // tt),
        in_specs=[
            pl.BlockSpec((1, d, tt), lambda bi, i: (bi, 0, i)),
            pl.BlockSpec((1, tt, d), lambda bi, i: (bi, i, 0)),
            pl.BlockSpec((1, 6, d), lambda bi, i: (bi, 0, 0)),
            pl.BlockSpec((1, d), lambda bi, i: (0, 0)),
        ],
        out_specs=pl.BlockSpec((1, tt, d), lambda bi, i: (bi, i, 0)),
        out_shape=SDS((b, t, d), F32),
        compiler_params=_params("parallel", "parallel"),
        name="ffn_residual",
    )(acc_t, x, mod_l, gpost)


def _rope_tables(seq):
    n_rows = seq // GRID_W
    row = jnp.repeat(jnp.arange(n_rows), GRID_W).astype(F32)
    col = jnp.tile(jnp.arange(GRID_W), n_rows).astype(F32)
    inv = 1.0 / (ROPE_THETA ** (jnp.arange(0, ROPE_HALF, 2, dtype=F32) / ROPE_HALF))
    ar = row[:, None] * inv[None]
    ac = col[:, None] * inv[None]
    ang = jnp.concatenate([ar, ar, ac, ac], axis=-1)
    cos, sin = jnp.cos(ang), jnp.sin(ang)
    first_half = (jnp.arange(HEAD_DIM) % ROPE_HALF) < (ROPE_HALF // 2)
    sin_signed = jnp.where(first_half[None], -sin, sin)
    return jnp.tile(cos, (1, 2)), jnp.tile(sin_signed, (1, 2))


def _filter_features(seq):
    pos = jnp.abs(jnp.arange(2 * seq) - seq).astype(F32)
    t = pos / (seq - 1)
    f = jnp.linspace(1e-4, HY_BANDS - 1, HY_BANDS, dtype=F32)
    ang = (2.0 * math.pi * pos / seq)[:, None] * f[None]
    z = jnp.concatenate([t[:, None], jnp.cos(ang), -jnp.sin(ang)], axis=-1)
    return jnp.pad(z, ((0, 0), (0, LANES - HY_EMB)))


def kernel(x, c, ada_w, ada_b, mix_pre_g, mix_post_g, w_in, gla_w_dec, gla_b_dec, gla_norm_g, q_norm_g, k_norm_g, hy_conv_w, hy_conv_b, hy_w1, hy_b1, hy_w2, hy_b2, hy_w3, hy_freq, hy_skip, w_out, ffn_pre_g, ffn_post_g, router_w, exp_w_gate, exp_w_up, exp_w_down):
    b, seq, d = x.shape
    depth = ada_w.shape[0]
    n_exp = router_w.shape[-1]
    cap = EC_FACTOR * seq // n_exp
    hw = hy_skip.shape[-1]
    nb = seq // HY_BLOCK
    assert seq % HY_BLOCK == 0 and (nb * b) % 8 == 0

    mod = _ada_call(c, ada_w, ada_b).reshape(depth, b, 6, d)
    cos2, sin2 = _rope_tables(seq)
    feat = _filter_features(seq)
    deltas = jnp.abs(jnp.linspace(HY_MIN_DECAY, HY_MAX_DECAY, hw, dtype=F32))[None]

    for l in range(depth):
        w = w_in[l]
        w_pack = jnp.concatenate(
            [w[:, :768], jnp.pad(w[:, 768:800], ((0, 0), (0, LANES - 2 * GLA_LOWRANK))), w[:, 800:]], axis=1
        ).astype(BF16)
        gqk, gv, gg, glr, aq, ak, av, hy = _inproj_call(
            x, mod[l], mix_pre_g[l][None], w_pack, cos2, sin2,
            jnp.tile(q_norm_g[l], 2)[None], jnp.tile(k_norm_g[l], 2)[None])
        o_att = _attn_call(aq, ak, av)

        wdec_pad = jnp.zeros((2, LANES, GLA_QK), F32)
        wdec_pad = wdec_pad.at[0, :GLA_LOWRANK].set(gla_w_dec[l, 0]).at[1, GLA_LOWRANK:2 * GLA_LOWRANK].set(gla_w_dec[l, 1])
        o_gla = _gla_call(gqk, gv, gg, glr, wdec_pad, gla_b_dec[l][:, None, :],
                          jnp.tile(gla_norm_g[l], GLA_HEADS)[None])

        z, x0 = _hy_pre_call(hy, hy_conv_w[l], hy_conv_b[l][None])
        w1p = jnp.pad(hy_w1[l], ((0, LANES - HY_EMB), (0, 0)))
        filt = _hy_filter_call(feat, w1p, hy_b1[l][None], hy_w2[l], hy_b2[l][:, None, :], hy_w3[l],
                               hy_freq[l][None], deltas, seq)
        ft = filt.T.reshape(hw, 2 * nb, HY_BLOCK)
        zt = jnp.transpose(z.reshape(b, nb, HY_BLOCK, hw), (3, 1, 0, 2)).reshape(hw, nb * b, HY_BLOCK)
        yt = _hy_conv_call(zt, ft, nb, b).reshape(hw, nb, b, HY_BLOCK)
        yc = jnp.transpose(yt, (2, 1, 3, 0)).reshape(b, seq, hw)

        rw_full = jnp.pad(router_w[l], ((0, 0), (0, LANES - n_exp)))
        rw_hi = rw_full.astype(BF16)
        rw_pad = jnp.stack([rw_hi, (rw_full - rw_hi.astype(F32)).astype(BF16)])
        x, h2, aff = _outproj_call(x, o_gla, o_att, yc, z, x0, hy_skip[l][None], w_out[l].astype(BF16), mod[l],
                                   mix_post_g[l][None], ffn_pre_g[l][None], rw_pad, n_exp)

        aff_t = jnp.swapaxes(aff[..., :n_exp], 1, 2)
        pos_rows, cum = _route_call(aff_t, cap)
        lo, hi = _chunk_ranges(cum, cap, min(EC_TOKEN_CHUNK, seq))
        aff_rows = aff_t.reshape(b * n_exp, 1, seq)
        xgt, gc = _dispatch_call(lo, hi, pos_rows, aff_rows, h2, n_exp, cap)
        yt = _ffn_call(xgt, gc, exp_w_gate, exp_w_up, exp_w_down, l)
        acc_t = _combine_call(lo, hi, pos_rows, yt, b, seq, cap)
        x = _ffn_residual_call(acc_t, x, mod[l], ffn_post_g[l][None])
    return x
```

```python
import functools
import math

import jax
import jax.numpy as jnp
from jax import lax
from jax.experimental import pallas as pl
from jax.experimental.pallas import tpu as pltpu

F32 = jnp.float32
BF16 = jnp.bfloat16
HI = lax.Precision.HIGHEST
SDS = jax.ShapeDtypeStruct

EPS = 1e-6
GRID_W = 64
HEAD_DIM = 64
GLA_HEADS = 4
GLA_DK = 32
GLA_DV = 64
GLA_QK = GLA_HEADS * GLA_DK
GLA_WIDTH = GLA_HEADS * GLA_DV
GLA_LOWRANK = 16
GLA_GATE_NORM = 16.0
GLA_BLOCK = 8
GLA_STATE_BLOCK = 64
GLA_STEP_UNROLL = 4
ATT_HEADS = 8
ATT_KV_HEADS = 2
ATT_GROUP = ATT_HEADS // ATT_KV_HEADS
ATT_WIDTH = ATT_HEADS * HEAD_DIM
ATT_KV = ATT_KV_HEADS * HEAD_DIM
ATT_TK = 256
ATT_ONES_ROWS = 16
ROPE_THETA = 10000.0
ROPE_HALF = HEAD_DIM // 2
HY_EMB = 33
HY_BANDS = (HY_EMB - 1) // 2
HY_TARGET = 1e-2
HY_FAST = 0.3
HY_SLOW = 1.5
HY_MIN_DECAY = math.log(HY_TARGET) / HY_SLOW
HY_MAX_DECAY = math.log(HY_TARGET) / HY_FAST
HY_BLOCK = 128
EC_FACTOR = 2
ROUTE_BISECTIONS = 160
EC_SLOT_BLOCK = 256
EC_TOKEN_CHUNK = 512
LANES = 128
SUBLANES = 8
VMEM_LIMIT = 56 * 1024 * 1024

PROJ_ROWS = 1024
OUT_SUB_ROWS = 256
ADA_COLS = 1024
ATT_TQ = 1024
GLA_ROWS = 256
HY_PRE_ROWS = 512
HY_FILTER_ROWS = 1024
HY_CONV_CHANNELS = 8
FFN_COLS = 256
FFN_ROWS = 512
RESID_ROWS = 512


def _params(*sem):
    return pltpu.CompilerParams(dimension_semantics=sem, vmem_limit_bytes=VMEM_LIMIT)


def _silu(x):
    return x * (1.0 / (1.0 + jnp.exp(-x)))


def _rms(x):
    return x * lax.rsqrt(jnp.mean(x * x, axis=-1, keepdims=True) + EPS)


def _ada_kernel(c_ref, w_ref, b_ref, o_ref):
    cond = _silu(c_ref[...])
    o_ref[0] = jnp.dot(cond, w_ref[0], precision=HI, preferred_element_type=F32) + b_ref[0]


def _ada_call(c, ada_w, ada_b):
    depth, d, n6 = ada_w.shape
    b = c.shape[0]
    tn = min(ADA_COLS, n6)
    return pl.pallas_call(
        _ada_kernel,
        grid=(depth, n6 // tn),
        in_specs=[
            pl.BlockSpec((b, d), lambda l, j: (0, 0)),
            pl.BlockSpec((1, d, tn), lambda l, j: (l, 0, j)),
            pl.BlockSpec((1, 1, tn), lambda l, j: (l, 0, j)),
        ],
        out_specs=pl.BlockSpec((1, b, tn), lambda l, j: (l, 0, j)),
        out_shape=SDS((depth, b, n6), F32),
        compiler_params=_params("arbitrary", "arbitrary"),
        name="ada_mod",
    )(c, ada_w, ada_b.reshape(depth, 1, n6))


_C_GQK = (0, 256)
_C_GV = (256, 512)
_C_GG = (512, 768)
_C_GLR = (768, 896)
_C_AQ = (896, 1408)
_C_AK = (1408, 1536)
_C_AV = (1536, 1664)
_C_HY = (1664, 2432)
_NPACK = 2432


def _head_mean_matrix():
    r = lax.broadcasted_iota(jnp.int32, (LANES, LANES), 0) // HEAD_DIM
    c = lax.broadcasted_iota(jnp.int32, (LANES, LANES), 1) // HEAD_DIM
    return jnp.where(r == c, 1.0 / HEAD_DIM, 0.0).astype(BF16)


def _norm_rope(a, gain, cos, sin_signed, gmat, first_half):
    ms = jnp.dot((a * a).astype(BF16), gmat, preferred_element_type=F32)
    xn = a * lax.rsqrt(ms + EPS) * gain
    rot = jnp.where(first_half, pltpu.roll(xn, LANES - ROPE_HALF // 2, 1), pltpu.roll(xn, ROPE_HALF // 2, 1))
    return xn * cos + rot * sin_signed


def _inproj_kernel(x_ref, mod_ref, g_ref, w_ref, cos_ref, sin_ref, qg_ref, kg_ref,
                   gqk_ref, gv_ref, gg_ref, glr_ref, q_ref, k_ref, vt_ref, hy_ref, *, tk):
    x = x_ref[0]
    sh = mod_ref[0, 0:1, :]
    sc = mod_ref[0, 1:2, :]
    h = (_rms(x) * g_ref[...] * (1.0 + sc) + sh).astype(BF16)

    def seg(c):
        return jnp.dot(h, w_ref[:, c[0]:c[1]], preferred_element_type=F32)

    gqk_ref[0] = seg(_C_GQK)
    gv_ref[0] = seg(_C_GV)
    gg_ref[0] = seg(_C_GG)
    glr_ref[0] = seg(_C_GLR)
    hy_ref[0] = seg(_C_HY)
    av = seg(_C_AV)
    for ci in range(av.shape[0] // tk):
        vt_ref[0, ci] = av[ci * tk:(ci + 1) * tk, :].T.astype(BF16)

    gmat = _head_mean_matrix()
    cos = cos_ref[...]
    sin_signed = sin_ref[...]
    lane = lax.broadcasted_iota(jnp.int32, cos.shape, 1)
    first_half = (lane % ROPE_HALF) < (ROPE_HALF // 2)
    aq = seg(_C_AQ)
    scale = HEAD_DIM ** -0.5
    for j in range(ATT_WIDTH // LANES):
        qj = _norm_rope(aq[:, j * LANES:(j + 1) * LANES], qg_ref[...], cos, sin_signed, gmat, first_half)
        q_ref[0, :, j * LANES:(j + 1) * LANES] = (qj * scale).astype(BF16)
    k_ref[0] = _norm_rope(seg(_C_AK), kg_ref[...], cos, sin_signed, gmat, first_half).astype(BF16)


def _inproj_call(x, mod_l, g, w_pack, cos2, sin2, qg2, kg2):
    b, l, d = x.shape
    tm = min(PROJ_ROWS, l)
    tk = min(ATT_TK, l)
    row = lambda bi, i: (bi, i, 0)
    full2 = lambda bi, i: (0, 0)
    rows_out = [
        (SDS((b, l, 256), F32), 256), (SDS((b, l, 256), F32), 256), (SDS((b, l, 256), F32), 256),
        (SDS((b, l, 128), F32), 128), (SDS((b, l, ATT_WIDTH), BF16), ATT_WIDTH),
        (SDS((b, l, ATT_KV), BF16), ATT_KV),
    ]
    out_specs = [pl.BlockSpec((1, tm, w), row) for _, w in rows_out]
    out_shape = [s for s, _ in rows_out]
    out_specs.append(pl.BlockSpec((1, tm // tk, ATT_KV, tk), lambda bi, i: (bi, i, 0, 0)))
    out_shape.append(SDS((b, l // tk, ATT_KV, tk), BF16))
    out_specs.append(pl.BlockSpec((1, tm, 768), row))
    out_shape.append(SDS((b, l, 768), F32))
    return pl.pallas_call(
        functools.partial(_inproj_kernel, tk=tk),
        grid=(b, l // tm),
        in_specs=[
            pl.BlockSpec((1, tm, d), row),
            pl.BlockSpec((1, 6, d), lambda bi, i: (bi, 0, 0)),
            pl.BlockSpec((1, d), full2),
            pl.BlockSpec((d, _NPACK), full2),
            pl.BlockSpec((tm, LANES), lambda bi, i: (i, 0)),
            pl.BlockSpec((tm, LANES), lambda bi, i: (i, 0)),
            pl.BlockSpec((1, LANES), full2),
            pl.BlockSpec((1, LANES), full2),
        ],
        out_specs=out_specs,
        out_shape=out_shape,
        compiler_params=_params("parallel", "parallel"),
        name="in_proj",
    )(x, mod_l, g, w_pack, cos2, sin2, qg2, kg2)


def _attn_kernel(q_ref, k_ref, vt_ref, o_ref, acc_ref, s_ref, p_ref, *, tq):
    nchunk = vt_ref.shape[1]
    tk = vt_ref.shape[3]
    lane = lax.broadcasted_iota(jnp.int32, (tq, LANES), 1)
    lower = lane < HEAD_DIM
    ones_rows = ATT_ONES_ROWS
    vrows = HEAD_DIM + ones_rows
    row = lax.broadcasted_iota(jnp.int32, (vrows, tk), 0)
    q = q_ref[0].astype(F32)
    streams = []
    qts = []
    for j in range(ATT_KV_HEADS):
        keep = lower if j == 0 else jnp.logical_not(lower)
        for hp in range(ATT_GROUP // 2):
            cb = j * (ATT_GROUP // 2) + hp
            qc = q[:, cb * LANES:(cb + 1) * LANES]
            qsw = pltpu.roll(qc, HEAD_DIM, 1)
            first, second = (qc, qsw) if j == 0 else (qsw, qc)
            cols = [jnp.where(keep, first, 0.0).T, jnp.where(keep, second, 0.0).T]
            qts.append(jnp.concatenate(cols, axis=1).astype(BF16))
            streams.append((j, cb))
    qt_all = jnp.concatenate(qts, axis=1)
    half = ATT_GROUP * tq
    acc_ref[...] = jnp.zeros(acc_ref.shape, F32)
    p_ref[1] = jnp.zeros(p_ref.shape[1:], BF16)
    ones = jnp.ones((vrows, tk), BF16)

    def scores(c):
        kc = k_ref[0, pl.ds(pl.multiple_of(c * tk, tk), tk), :]
        return jnp.dot(kc, qt_all, preferred_element_type=F32)

    def add_values(c, slot, alpha):
        vt = vt_ref[0, c]
        for j in range(ATT_KV_HEADS):
            r0 = j * (LANES - vrows)
            vm = jnp.where((row < HEAD_DIM) if j == 0 else (row >= ones_rows), vt[r0:r0 + vrows, :], ones)
            pj = p_ref[slot, :, j * half:(j + 1) * half]
            acc_ref[j] = acc_ref[j] * alpha[:, j * half:(j + 1) * half] + jnp.dot(vm, pj, preferred_element_type=F32)

    s_ref[0] = scores(0)

    def step(c, slot, carry):
        m_prev, alpha_prev = carry
        s_ref[1 - slot] = scores(jnp.minimum(c + 1, nchunk - 1))
        add_values(jnp.maximum(c - 1, 0), 1 - slot, alpha_prev)
        st = s_ref[slot]
        m_new = jnp.maximum(m_prev, jnp.max(st, axis=0, keepdims=True))
        p_ref[slot] = jnp.exp((st - m_new).astype(BF16))
        return m_new, jnp.exp(m_prev - m_new)

    def body(i, carry):
        return step(2 * i + 1, 1, step(2 * i, 0, carry))

    assert nchunk % 2 == 0
    init = (jnp.full((1, 2 * half), -jnp.inf, F32), jnp.ones((1, 2 * half), F32))
    _, alpha = lax.fori_loop(0, nchunk // 2, body, init)
    add_values(nchunk - 1, 1, alpha)

    for j, cb in streams:
        hp = cb % (ATT_GROUP // 2)
        acc = acc_ref[j][:, hp * 2 * tq:(hp + 1) * 2 * tq]
        d0 = j * ones_rows
        denom = acc[(1 - j) * HEAD_DIM:(1 - j) * HEAD_DIM + 1, :]
        data = acc[d0:d0 + HEAD_DIM, :] / denom
        blk = jnp.concatenate([data[:, :tq], data[:, tq:]], axis=0)
        o_ref[0, :, cb * LANES:(cb + 1) * LANES] = blk.T.astype(BF16)


def _attn_call(q, k, vt):
    b, l, _ = q.shape
    tq = min(ATT_TQ, l)
    nchunk, tk = vt.shape[1], vt.shape[3]
    nq = ATT_HEADS * tq
    return pl.pallas_call(
        functools.partial(_attn_kernel, tq=tq),
        grid=(b, l // tq),
        in_specs=[
            pl.BlockSpec((1, tq, ATT_WIDTH), lambda bi, i: (bi, i, 0)),
            pl.BlockSpec((1, l, ATT_KV), lambda bi, i: (bi, 0, 0)),
            pl.BlockSpec((1, nchunk, ATT_KV, tk), lambda bi, i: (bi, 0, 0, 0)),
        ],
        out_specs=pl.BlockSpec((1, tq, ATT_WIDTH), lambda bi, i: (bi, i, 0)),
        out_shape=SDS((b, l, ATT_WIDTH), BF16),
        scratch_shapes=[pltpu.VMEM((ATT_KV_HEADS, HEAD_DIM + ATT_ONES_ROWS, nq // ATT_KV_HEADS), F32),
                        pltpu.VMEM((2, tk, nq), F32), pltpu.VMEM((2, tk, nq), BF16)],
        compiler_params=_params("parallel", "parallel"),
        name="gqa_attention",
    )(q, k, vt)


def _log_sigmoid(x):
    return jnp.minimum(x, 0.0) - jnp.log(1.0 + jnp.exp(-jnp.abs(x)))


def _gla_kernel(qk_ref, v_ref, gg_ref, lr_ref, wdec_ref, bdec_ref, ng_ref, o_ref,
                qs_ref, ks_ref, a_ref, od_ref, *, rt):
    seq = qk_ref.shape[1]
    cb = GLA_BLOCK
    sb = GLA_STATE_BLOCK
    rowi = lax.broadcasted_iota(jnp.int32, (rt, GLA_QK), 0)
    rowmod = rowi % cb
    hd = lax.broadcasted_iota(jnp.int32, (GLA_QK, GLA_WIDTH), 0) // GLA_DK
    hc = lax.broadcasted_iota(jnp.int32, (GLA_QK, GLA_WIDTH), 1) // GLA_DV
    expand = jnp.where(hd == hc, 1.0, 0.0).astype(BF16)
    qk_head = lax.broadcasted_iota(jnp.int32, (rt, GLA_QK), 1) // GLA_DK
    v_head = lax.broadcasted_iota(jnp.int32, (rt, GLA_WIDTH), 1) // GLA_DV
    pair_i = lax.broadcasted_iota(jnp.int32, (rt, rt), 0)
    pair_j = lax.broadcasted_iota(jnp.int32, (rt, rt), 1)
    scale = GLA_DK ** -0.5

    def prep(t, _):
        r0 = pl.multiple_of(t * rt, rt)
        rows = pl.ds(r0, rt)
        q = qk_ref[0, rows, 0:GLA_QK] * scale
        k = qk_ref[0, rows, GLA_QK:2 * GLA_QK]
        v = v_ref[0, rows, :]
        lr = lr_ref[0, rows, :]
        od = jnp.zeros((rt, GLA_WIDTH), F32)
        half_sums = []
        for direction in range(2):
            la = _log_sigmoid(jnp.dot(lr, wdec_ref[direction], precision=HI, preferred_element_type=F32)
                              + bdec_ref[direction]) * (1.0 / GLA_GATE_NORM)
            pre = la
            suf = la
            s = 1
            while s < cb:
                pre = pre + jnp.where(rowmod >= s, pltpu.roll(pre, s, 0), 0.0)
                suf = suf + jnp.where(rowmod < cb - s, pltpu.roll(suf, rt - s, 0), 0.0)
                s *= 2
            tot = pre + suf - la
            sums = {}
            size = cb
            while size < sb:
                sums[size] = (pre, suf, la)
                upper = (rowi % (2 * size)) >= size
                t_below = pltpu.roll(tot, size, 0)
                t_above = pltpu.roll(tot, rt - size, 0)
                pre = pre + jnp.where(upper, t_below, 0.0)
                suf = suf + jnp.where(upper, 0.0, t_above)
                tot = tot + jnp.where(upper, t_below, t_above)
                size *= 2
            half_sums.append(sums)
            cum = pre if direction == 0 else suf
            rem = (suf if direction == 0 else pre) - la
            qs_ref[direction, rows, :] = q * jnp.exp(cum)
            ks_ref[direction, rows, :] = k * jnp.exp(rem)
            a_ref[direction, rows, :] = jnp.exp(tot)
            for delta in range(cb):
                if delta == 0:
                    a_mat = q * k
                    vj = v
                else:
                    if direction == 0:
                        sh = delta
                        valid = rowmod >= delta
                    else:
                        sh = rt - delta
                        valid = rowmod < cb - delta
                    kj = pltpu.roll(k, sh, 0)
                    cj = pltpu.roll(cum, sh, 0)
                    vj = pltpu.roll(v, sh, 0)
                    a_mat = jnp.where(valid, q * kj * jnp.exp(cum - cj), 0.0)
                se = jnp.dot(a_mat.astype(BF16), expand, preferred_element_type=F32)
                od = od + se * vj
        size = cb
        while size < sb:
            upper = (rowi % (2 * size)) >= size
            pre_f, suf_f, la_f = half_sums[0][size]
            pre_b, suf_b, la_b = half_sums[1][size]
            qf = (q * jnp.exp(jnp.where(upper, pre_f, suf_b))).astype(BF16)
            kf = (k * jnp.exp(jnp.where(upper, pre_b - la_b, suf_f - la_f))).astype(BF16)
            same_block = (pair_i // (2 * size)) == (pair_j // (2 * size))
            cross = ((pair_i % (2 * size)) >= size) != ((pair_j % (2 * size)) >= size)
            keep = jnp.logical_and(same_block, cross)
            for h in range(GLA_HEADS):
                qh = jnp.where(qk_head == h, qf, jnp.zeros_like(qf))
                sc = lax.dot_general(qh, kf, (((1,), (1,)), ((), ())), preferred_element_type=F32)
                vh = jnp.where(v_head == h, v, 0.0).astype(BF16)
                od = od + jnp.dot(jnp.where(keep, sc, 0.0).astype(BF16), vh, preferred_element_type=F32)
            size *= 2
        od_ref[rows, :] = od
        return 0

    lax.fori_loop(0, seq // rt, prep, 0)

    sr = lax.broadcasted_iota(jnp.int32, (GLA_WIDTH, GLA_QK), 0) // GLA_DV
    scol = lax.broadcasted_iota(jnp.int32, (GLA_WIDTH, GLA_QK), 1) // GLA_DK
    same_head = sr == scol
    nblk = seq // sb
    unroll = min(GLA_STEP_UNROLL, nblk)
    span = unroll * sb

    def step(i, states):
        states = list(states)
        for direction in range(2):
            base = i * span if direction == 0 else seq - (i + 1) * span
            rows = pl.ds(pl.multiple_of(base, span), span)
            qb = qs_ref[direction, rows, :].astype(BF16)
            kb = ks_ref[direction, rows, :].astype(BF16)
            vb = v_ref[0, rows, :].astype(BF16)
            ab = a_ref[direction, rows, :]
            st = states[direction]
            outs = [None] * unroll
            for u in (range(unroll) if direction == 0 else range(unroll - 1, -1, -1)):
                sl = slice(u * sb, (u + 1) * sb)
                outs[u] = lax.dot_general(qb[sl], st.astype(BF16), (((1,), (1,)), ((), ())),
                                          preferred_element_type=F32)
                kv = lax.dot_general(vb[sl], kb[sl], (((0,), (0,)), ((), ())), preferred_element_type=F32)
                st = st * ab[u * sb:u * sb + 1, :] + jnp.where(same_head, kv, 0.0)
            od_ref[rows, :] += jnp.concatenate(outs, axis=0)
            states[direction] = st
        return tuple(states)

    zero_state = jnp.zeros((GLA_WIDTH, GLA_QK), F32)
    lax.fori_loop(0, nblk // unroll, step, (zero_state, zero_state))

    gr = lax.broadcasted_iota(jnp.int32, (GLA_WIDTH, GLA_WIDTH), 0) // GLA_DV
    gc = lax.broadcasted_iota(jnp.int32, (GLA_WIDTH, GLA_WIDTH), 1) // GLA_DV
    gmat = jnp.where(gr == gc, 1.0 / GLA_DV, 0.0).astype(BF16)

    def fin(t, _):
        rows = pl.ds(pl.multiple_of(t * rt, rt), rt)
        o = od_ref[rows, :]
        ms = jnp.dot((o * o).astype(BF16), gmat, preferred_element_type=F32)
        o = o * lax.rsqrt(ms + EPS) * ng_ref[...]
        o_ref[0, rows, :] = (o * _silu(gg_ref[0, rows, :])).astype(BF16)
        return 0

    lax.fori_loop(0, seq // rt, fin, 0)


def _gla_call(gqk, gv, gg, glr, wdec_pad, bdec, ng4):
    b, l, _ = gqk.shape
    rt = min(GLA_ROWS, l)
    once = dict(pipeline_mode=pl.Buffered(1))
    bmap = lambda bi: (bi, 0, 0)
    return pl.pallas_call(
        functools.partial(_gla_kernel, rt=rt),
        grid=(b,),
        in_specs=[
            pl.BlockSpec((1, l, 2 * GLA_QK), bmap, **once),
            pl.BlockSpec((1, l, GLA_WIDTH), bmap, **once),
            pl.BlockSpec((1, l, GLA_WIDTH), bmap, **once),
            pl.BlockSpec((1, l, LANES), bmap, **once),
            pl.BlockSpec((2, LANES, GLA_QK), lambda bi: (0, 0, 0)),
            pl.BlockSpec((2, 1, GLA_QK), lambda bi: (0, 0, 0)),
            pl.BlockSpec((1, GLA_WIDTH), lambda bi: (0, 0)),
        ],
        out_specs=pl.BlockSpec((1, l, GLA_WIDTH), bmap),
        out_shape=SDS((b, l, GLA_WIDTH), BF16),
        scratch_shapes=[
            pltpu.VMEM((2, l, GLA_QK), F32),
            pltpu.VMEM((2, l, GLA_QK), F32),
            pltpu.VMEM((2, l, GLA_QK), F32),
            pltpu.VMEM((l, GLA_WIDTH), F32),
        ],
        compiler_params=_params("parallel"),
        name="gla",
    )(gqk, gv, gg, glr, wdec_pad, bdec, ng4)


def _hy_pre_kernel(x0_ref, x1_ref, v_ref, w0_ref, w1_ref, wv_ref, b0_ref, b1_ref, bv_ref, z_ref, x0o_ref, *, rt):
    seq = x0_ref.shape[1]
    rowid = lax.broadcasted_iota(jnp.int32, (rt, LANES), 0)

    def conv(ref, w_ref, b_ref, r0):
        cur = ref[0, pl.ds(r0, rt), :]
        g = SUBLANES
        prev = ref[0, pl.ds(pl.multiple_of(jnp.maximum(r0 - g, 0), g), g), :][g - 1:g, :]
        nxt = ref[0, pl.ds(pl.multiple_of(jnp.minimum(r0 + rt, seq - g), g), g), :][0:1, :]
        prev = jnp.where(r0 > 0, prev, 0.0)
        nxt = jnp.where(r0 + rt < seq, nxt, 0.0)
        up = jnp.where(rowid == 0, prev, pltpu.roll(cur, 1, 0))
        down = jnp.where(rowid == rt - 1, nxt, pltpu.roll(cur, rt - 1, 0))
        return up * w_ref[0:1, :] + cur * w_ref[1:2, :] + down * w_ref[2:3, :] + b_ref[...]

    def body(t, _):
        r0 = pl.multiple_of(t * rt, rt)
        rows = pl.ds(r0, rt)
        x0o_ref[0, rows, :] = conv(x0_ref, w0_ref, b0_ref, r0)
        z_ref[0, rows, :] = conv(v_ref, wv_ref, bv_ref, r0) * conv(x1_ref, w1_ref, b1_ref, r0)
        return 0

    lax.fori_loop(0, seq // rt, body, 0)


def _hy_pre_call(hy, conv_w, conv_b):
    b, l, w3 = hy.shape
    hw = w3 // 3
    nj = hw // LANES
    rt = min(HY_PRE_ROWS, l)
    xs = lambda off: pl.BlockSpec((1, l, LANES), lambda bi, j: (bi, 0, off * nj + j))
    ws = lambda off: pl.BlockSpec((3, LANES), lambda bi, j: (0, off * nj + j))
    bs = lambda off: pl.BlockSpec((1, LANES), lambda bi, j: (0, off * nj + j))
    out = pl.BlockSpec((1, l, LANES), lambda bi, j: (bi, 0, j))
    return pl.pallas_call(
        functools.partial(_hy_pre_kernel, rt=rt),
        grid=(b, nj),
        in_specs=[xs(0), xs(1), xs(2), ws(0), ws(1), ws(2), bs(0), bs(1), bs(2)],
        out_specs=[out, out],
        out_shape=[SDS((b, l, hw), F32), SDS((b, l, hw), F32)],
        compiler_params=_params("parallel", "parallel"),
        name="hyena_short_conv",
    )(hy, hy, hy, conv_w, conv_w, conv_w, conv_b, conv_b, conv_b)


def _hy_filter_kernel(z_ref, w1_ref, b1_ref, w2_ref, b2_ref, w3_ref, fr_ref, dl_ref, o_ref, *, seq, tr):
    half = tr // 2
    zs = (z_ref[0:half, :], z_ref[half:tr, :])
    fr = fr_ref[...]
    pre = (jnp.dot(zs[0], w1_ref[0], precision=HI, preferred_element_type=F32)
           + jnp.dot(zs[1], w1_ref[1], precision=HI, preferred_element_type=F32))
    h = jnp.sin(fr * (pre + b1_ref[...]))
    for i in range(w2_ref.shape[0]):
        h = jnp.sin(fr * (jnp.dot(h, w2_ref[i], precision=HI, preferred_element_type=F32) + b2_ref[i]))
    c = dl_ref.shape[1]
    for part in range(2):
        hw = jnp.dot(h, w3_ref[part], precision=HI, preferred_element_type=F32)
        window = jnp.exp(-zs[part][:, 0:1] * dl_ref[...])
        m = pl.program_id(0) * tr + part * half + lax.broadcasted_iota(jnp.int32, (half, c), 0)
        tap = jnp.where(m >= seq, hw[:, :c], hw[:, c:]) * window
        o_ref[part * half:(part + 1) * half, :] = jnp.where(m == 0, 0.0, tap)


def _hy_filter_call(feat, w1p, b1, w2, b2, w3, freq, deltas, seq):
    n2, _ = feat.shape
    od = w1p.shape[1]
    c2 = w3.shape[1]
    tr = min(HY_FILTER_ROWS, n2)
    zero = lambda a: jnp.zeros_like(a)
    w1_2 = jnp.stack([jnp.concatenate([w1p, zero(w1p)], 1), jnp.concatenate([zero(w1p), w1p], 1)])
    w2_2 = jnp.concatenate([jnp.concatenate([w2, zero(w2)], 2), jnp.concatenate([zero(w2), w2], 2)], 1)
    w3_2 = jnp.stack([jnp.concatenate([w3, zero(w3)], 0), jnp.concatenate([zero(w3), w3], 0)])
    twice = lambda a: jnp.concatenate([a, a], -1)
    full = lambda *shape: pl.BlockSpec(shape, lambda i: (0,) * len(shape))
    return pl.pallas_call(
        functools.partial(_hy_filter_kernel, seq=seq, tr=tr),
        grid=(n2 // tr,),
        in_specs=[
            pl.BlockSpec((tr, LANES), lambda i: (i, 0)),
            full(2, LANES, 2 * od), full(1, 2 * od), full(w2.shape[0], 2 * od, 2 * od), full(w2.shape[0], 1, 2 * od),
            full(2, 2 * od, c2), full(1, 2 * od), full(1, c2 // 2),
        ],
        out_specs=pl.BlockSpec((tr, c2 // 2), lambda i: (i, 0)),
        out_shape=SDS((n2, c2 // 2), F32),
        compiler_params=_params("parallel"),
        name="hyena_filter",
    )(feat, w1_2, twice(b1), w2_2, twice(b2), w3_2, twice(freq), deltas)


def _hy_conv_kernel(z_ref, f_ref, y_ref, acc_ref, *, nb, bp):
    cg = z_ref.shape[0]
    p = HY_BLOCK
    ii = lax.broadcasted_iota(jnp.int32, (p, p), 0)
    jj = lax.broadcasted_iota(jnp.int32, (p, p), 1)
    upper = jj >= ii

    rows = nb * bp
    sub = SUBLANES
    shifts = sorted({(bp * d) % sub for d in range(-(nb - 1), nb)})

    def chan(ci, _):
        pad = jnp.zeros((sub, p), F32)
        zfull = jnp.concatenate([pad, z_ref[ci], pad], axis=0)
        zs = {m: (zfull if m == 0 else pltpu.roll(zfull, m, 0)) for m in shifts}
        acc_ref[...] = jnp.zeros(acc_ref.shape, F32)

        def skew(k):
            taps = jnp.broadcast_to(f_ref[ci, k:k + 1, :], (p, p))
            return pltpu.roll(taps, 0, 1, stride=1, stride_axis=0).astype(BF16)

        hi = skew(0)
        for k in range(2 * nb - 1):
            delta = k - (nb - 1)
            lo, hi = hi, skew(k + 1)
            w = jnp.where(upper, hi, lo)
            s = bp * delta
            o0 = max(0, s) // sub * sub
            o1 = -(-min(rows, rows + s) // sub) * sub
            m = s % sub
            src = o0 - s + sub + m
            lhs = zs[m][src:src + (o1 - o0), :].astype(BF16)
            acc_ref[o0:o1, :] += jnp.dot(lhs, w, preferred_element_type=F32)
        y_ref[ci] = acc_ref[...]
        return 0

    lax.fori_loop(0, cg, chan, 0)


def _hy_conv_call(zt, ft, nb, bp):
    c, rows, p = zt.shape
    cg = min(HY_CONV_CHANNELS, c)
    return pl.pallas_call(
        functools.partial(_hy_conv_kernel, nb=nb, bp=bp),
        grid=(c // cg,),
        in_specs=[
            pl.BlockSpec((cg, rows, p), lambda g: (g, 0, 0)),
            pl.BlockSpec((cg, 2 * nb, p), lambda g: (g, 0, 0)),
        ],
        out_specs=pl.BlockSpec((cg, rows, p), lambda g: (g, 0, 0)),
        out_shape=SDS((c, rows, p), F32),
        scratch_shapes=[pltpu.VMEM((rows, p), F32)],
        compiler_params=_params("parallel"),
        name="hyena_long_conv",
    )(zt, ft)


def _outproj_kernel(x_ref, ogla_ref, oatt_ref, yc_ref, z_ref, x0_ref, skip_ref, w_ref, mod_ref,
                    gpost_ref, gpre_ref, rw_ref, xo_ref, h2_ref, aff_ref, *, n_exp, sub):
    g0 = GLA_WIDTH
    g1 = GLA_WIDTH + ATT_WIDTH
    g_m = mod_ref[0, 2:3, :]
    sh_f = mod_ref[0, 3:4, :]
    sc_f = mod_ref[0, 4:5, :]
    for s in range(x_ref.shape[1] // sub):
        rows = slice(s * sub, (s + 1) * sub)
        ohy = (x0_ref[0, rows, :] * (yc_ref[0, rows, :] + z_ref[0, rows, :] * skip_ref[...])).astype(BF16)
        y = jnp.dot(ogla_ref[0, rows, :], w_ref[0:g0, :], preferred_element_type=F32)
        y = y + jnp.dot(oatt_ref[0, rows, :], w_ref[g0:g1, :], preferred_element_type=F32)
        y = y + jnp.dot(ohy, w_ref[g1:, :], preferred_element_type=F32)
        x1 = x_ref[0, rows, :] + g_m * (_rms(y) * gpost_ref[...])
        xo_ref[0, rows, :] = x1
        h2 = _rms(x1) * gpre_ref[...] * (1.0 + sc_f) + sh_f
        h2_ref[0, :, rows] = h2.T.astype(BF16)
        h_hi = h2.astype(BF16)
        h_lo = (h2 - h_hi.astype(F32)).astype(BF16)
        logits = (jnp.dot(h_hi, rw_ref[0], preferred_element_type=F32)
                  + jnp.dot(h_lo, rw_ref[0], preferred_element_type=F32)
                  + jnp.dot(h_hi, rw_ref[1], preferred_element_type=F32))
        lane = lax.broadcasted_iota(jnp.int32, logits.shape, 1)
        logits = jnp.where(lane < n_exp, logits, -jnp.inf)
        e = jnp.exp(logits - jnp.max(logits, axis=-1, keepdims=True))
        aff_ref[0, rows, :] = e / jnp.sum(e, axis=-1, keepdims=True)


def _outproj_call(x, ogla, oatt, yc, z, x0, skip, w_out, mod_l, gpost, gpre, rw_pad, n_exp):
    b, l, d = x.shape
    tm = min(PROJ_ROWS, l)
    row = lambda bi, i: (bi, i, 0)
    full2 = lambda bi, i: (0, 0)
    hw = yc.shape[-1]
    return pl.pallas_call(
        functools.partial(_outproj_kernel, n_exp=n_exp, sub=min(OUT_SUB_ROWS, tm)),
        grid=(b, l // tm),
        in_specs=[
            pl.BlockSpec((1, tm, d), row),
            pl.BlockSpec((1, tm, GLA_WIDTH), row),
            pl.BlockSpec((1, tm, ATT_WIDTH), row),
            pl.BlockSpec((1, tm, hw), row),
            pl.BlockSpec((1, tm, hw), row),
            pl.BlockSpec((1, tm, hw), row),
            pl.BlockSpec((1, hw), full2),
            pl.BlockSpec((d, d), full2),
            pl.BlockSpec((1, 6, d), lambda bi, i: (bi, 0, 0)),
            pl.BlockSpec((1, d), full2),
            pl.BlockSpec((1, d), full2),
            pl.BlockSpec((2, d, LANES), lambda bi, i: (0, 0, 0)),
        ],
        out_specs=[pl.BlockSpec((1, tm, d), row), pl.BlockSpec((1, d, tm), lambda bi, i: (bi, 0, i)),
                   pl.BlockSpec((1, tm, LANES), row)],
        out_shape=[SDS((b, l, d), F32), SDS((b, d, l), BF16), SDS((b, l, LANES), F32)],
        compiler_params=_params("parallel", "parallel"),
        name="out_proj_router",
    )(x, ogla, oatt, yc, z, x0, skip, w_out, mod_l, gpost, gpre, rw_pad)


def _route_kernel(aff_ref, pos_ref, cum_ref, *, cap):
    aff = aff_ref[0]
    n_exp, t = aff.shape

    def bisect(state):
        i, lo, hi, _ = state
        mid = 0.5 * (lo + hi)
        ok = jnp.sum(jnp.where(aff >= mid, 1.0, 0.0), axis=1, keepdims=True) >= cap
        moving = jnp.max(jnp.where(jnp.logical_and(mid > lo, mid < hi), 1, 0))
        return i + 1, jnp.where(ok, mid, lo), jnp.where(ok, hi, mid), moving

    def unsettled(state):
        i, _, _, moving = state
        return jnp.logical_and(i < ROUTE_BISECTIONS, moving > 0)

    _, lo, hi, _ = lax.while_loop(
        unsettled, bisect,
        (jnp.int32(0), jnp.zeros((n_exp, 1), F32), jnp.full((n_exp, 1), 2.0, F32), jnp.int32(1)))
    need = cap - jnp.sum(jnp.where(aff >= hi, 1.0, 0.0), axis=1, keepdims=True)
    r = lax.broadcasted_iota(jnp.int32, (LANES, LANES), 0)
    c = lax.broadcasted_iota(jnp.int32, (LANES, LANES), 1)
    tri = jnp.where(r <= c, 1.0, 0.0).astype(BF16)
    eq_run = jnp.zeros((n_exp, 1), F32)
    sel_run = jnp.zeros((n_exp, 1), F32)
    runs = []
    for j in range(t // LANES):
        sl = slice(j * LANES, (j + 1) * LANES)
        aj = aff[:, sl]
        eq_j = jnp.where(aj >= hi, 0.0, jnp.where(aj >= lo, 1.0, 0.0))
        eq_cum = jnp.dot(eq_j.astype(BF16), tri, preferred_element_type=F32) + eq_run
        sel_j = jnp.where(aj >= hi, 1.0, jnp.where(eq_cum <= need, eq_j, 0.0))
        sel_cum = jnp.dot(sel_j.astype(BF16), tri, preferred_element_type=F32) + sel_run
        pos_ref[0, :, sl] = jnp.where(sel_j > 0.0, sel_cum - 1.0, -1.0).astype(jnp.int32)
        eq_run = eq_cum[:, LANES - 1:LANES]
        sel_run = sel_cum[:, LANES - 1:LANES]
        runs.append(sel_run)
    cum_ref[0] = jnp.concatenate(runs, axis=1).astype(jnp.int32)


def _route_call(aff_t, cap):
    b, n_exp, t = aff_t.shape
    rows = b * n_exp
    pos, cum = pl.pallas_call(
        functools.partial(_route_kernel, cap=cap),
        grid=(1,),
        in_specs=[pl.BlockSpec((1, rows, t), lambda i: (0, 0, 0))],
        out_specs=[pl.BlockSpec((1, rows, t), lambda i: (0, 0, 0)),
                   pl.BlockSpec((1, rows, t // LANES), lambda i: (0, 0, 0))],
        out_shape=[SDS((1, rows, t), jnp.int32), SDS((1, rows, t // LANES), jnp.int32)],
        compiler_params=_params("arbitrary"),
        name="ec_route",
    )(aff_t.reshape(1, rows, t))
    return pos.reshape(rows, 1, t), cum.reshape(rows, t // LANES)


def _chunk_ranges(cum, cap, tc):
    per = tc // LANES
    cum_c = cum[:, per - 1::per]
    first = jnp.arange(0, cap, EC_SLOT_BLOCK, dtype=jnp.int32)
    chunk_of = lambda s: jnp.sum(cum_c[:, None, :] <= s[None, :, None], axis=-1).astype(jnp.int32)
    return chunk_of(first).reshape(-1), chunk_of(first + EC_SLOT_BLOCK - 1).reshape(-1)


def _one_hot(pos_ref, off, tc, first_slot):
    slot = first_slot + lax.broadcasted_iota(jnp.int32, (EC_SLOT_BLOCK, tc), 0)
    return jnp.where(slot == pos_ref[0, :, pl.ds(off, tc)], 1.0, 0.0).astype(BF16)


def _dispatch_kernel(lo_ref, hi_ref, pos_ref, aff_ref, ht_ref, xgt_ref, gc_ref, acc_ref, gacc_ref, *, tc, no_slot):
    nr = pl.num_programs(2)
    r = pl.program_id(2)
    idx = (pl.program_id(0) * pl.num_programs(1) + pl.program_id(1)) * nr + r
    acc_ref[...] = jnp.zeros(acc_ref.shape, F32)
    gacc_ref[...] = jnp.zeros(gacc_ref.shape, F32)
    nt = (((1,), (1,)), ((), ()))

    lo = lo_ref[idx]
    hi = hi_ref[idx]

    def body(i, _):
        c0 = lo + 2 * i
        chunks = ((c0, r * EC_SLOT_BLOCK),
                  (jnp.minimum(c0 + 1, hi), jnp.where(c0 + 1 <= hi, r * EC_SLOT_BLOCK, no_slot)))
        part = gpart = None
        for c, first_slot in chunks:
            off = pl.multiple_of(c * tc, tc)
            sel = _one_hot(pos_ref, off, tc, first_slot)
            d = lax.dot_general(ht_ref[0, :, pl.ds(off, tc)], sel, nt, preferred_element_type=F32)
            g = aff_ref[0, :, pl.ds(off, tc)]
            g1 = g.astype(BF16).astype(F32)
            g2 = (g - g1).astype(BF16).astype(F32)
            g3 = g - g1 - g2
            prow = lax.broadcasted_iota(jnp.int32, (gacc_ref.shape[0], tc), 0)
            pieces = jnp.where(prow == 0, g1, jnp.where(prow == 1, g2, jnp.where(prow == 2, g3, 0.0))).astype(BF16)
            gd = lax.dot_general(pieces, sel, nt, preferred_element_type=F32)
            part = d if part is None else part + d
            gpart = gd if gpart is None else gpart + gd
        acc_ref[...] += part
        gacc_ref[...] += gpart
        return 0

    lax.fori_loop(0, (hi - lo + 2) // 2, body, 0)
    xgt_ref[0] = acc_ref[...].astype(BF16)
    gc_ref[0] = jnp.sum(gacc_ref[...], axis=0, keepdims=True)


def _dispatch_call(lo, hi, pos_rows, aff_rows, h2t, n_exp, cap):
    b, d, t = h2t.shape
    tc = min(EC_TOKEN_CHUNK, t)
    nr = cap // EC_SLOT_BLOCK
    row = lambda bi, e, r, lo_r, hi_r: (bi * n_exp + e, 0, 0)
    out = lambda bi, e, r, lo_r, hi_r: (e, 0, bi * nr + r)
    return pl.pallas_call(
        functools.partial(_dispatch_kernel, tc=tc, no_slot=cap),
        grid_spec=pltpu.PrefetchScalarGridSpec(
            num_scalar_prefetch=2,
            grid=(b, n_exp, nr),
            in_specs=[
                pl.BlockSpec((1, 1, t), row),
                pl.BlockSpec((1, 1, t), row),
                pl.BlockSpec((1, d, t), lambda bi, e, r, lo_r, hi_r: (bi, 0, 0)),
            ],
            out_specs=[pl.BlockSpec((1, d, EC_SLOT_BLOCK), out), pl.BlockSpec((1, 1, EC_SLOT_BLOCK), out)],
            scratch_shapes=[pltpu.VMEM((d, EC_SLOT_BLOCK), F32), pltpu.VMEM((2 * SUBLANES, EC_SLOT_BLOCK), F32)],
        ),
        out_shape=[SDS((n_exp, d, b * cap), BF16), SDS((n_exp, 1, b * cap), F32)],
        compiler_params=_params("parallel", "arbitrary", "arbitrary"),
        name="ec_dispatch",
    )(lo, hi, pos_rows, aff_rows, h2t)


def _ffn_kernel(xgt_ref, gc_ref, wg_ref, wu_ref, wd_ref, yt_ref, xg_ref, acc_ref, *, mt):
    f = pl.program_id(1)
    m = xg_ref.shape[0]

    @pl.when(f == 0)
    def _():
        acc_ref[...] = jnp.zeros(acc_ref.shape, F32)
        for i in range(m // mt):
            xg_ref[i * mt:(i + 1) * mt, :] = xgt_ref[0, :, i * mt:(i + 1) * mt].T

    wg = wg_ref[0, 0].astype(BF16)
    wu = wu_ref[0, 0].astype(BF16)
    wd = wd_ref[0, 0].astype(BF16)
    for i in range(m // mt):
        rows = slice(i * mt, (i + 1) * mt)
        xb = xg_ref[rows, :]
        a = jnp.dot(xb, wg, preferred_element_type=F32)
        u = jnp.dot(xb, wu, preferred_element_type=F32)
        acc_ref[rows, :] += jnp.dot((_silu(a) * u).astype(BF16), wd, preferred_element_type=F32)

    @pl.when(f == pl.num_programs(1) - 1)
    def _():
        for i in range(m // mt):
            cols = slice(i * mt, (i + 1) * mt)
            yt_ref[0, :, cols] = (acc_ref[cols, :].T * gc_ref[0, :, cols]).astype(BF16)


def _ffn_call(xgt, gc, w_gate, w_up, w_down, layer):
    n_exp, d, m = xgt.shape
    ff = w_gate.shape[3]
    tf = min(FFN_COLS, ff)
    mt = min(FFN_ROWS, m)
    return pl.pallas_call(
        functools.partial(_ffn_kernel, mt=mt),
        grid=(n_exp, ff // tf),
        in_specs=[
            pl.BlockSpec((1, d, m), lambda e, f: (e, 0, 0)),
            pl.BlockSpec((1, 1, m), lambda e, f: (e, 0, 0)),
            pl.BlockSpec((1, 1, d, tf), lambda e, f: (layer, e, 0, f)),
            pl.BlockSpec((1, 1, d, tf), lambda e, f: (layer, e, 0, f)),
            pl.BlockSpec((1, 1, tf, d), lambda e, f: (layer, e, f, 0)),
        ],
        out_specs=pl.BlockSpec((1, d, m), lambda e, f: (e, 0, 0)),
        out_shape=SDS((n_exp, d, m), BF16),
        scratch_shapes=[pltpu.VMEM((m, d), BF16), pltpu.VMEM((m, d), F32)],
        compiler_params=_params("parallel", "arbitrary"),
        name="expert_ffn",
    )(xgt, gc, w_gate, w_up, w_down)


def _combine_kernel(lo_ref, hi_ref, pos_ref, yt_ref, o_ref, *, tc, no_slot):
    ne = pl.num_programs(1)
    nr = pl.num_programs(2)
    e = pl.program_id(1)
    r = pl.program_id(2)
    idx = (pl.program_id(0) * ne + e) * nr + r

    @pl.when(jnp.logical_and(e == 0, r == 0))
    def _():
        o_ref[...] = jnp.zeros(o_ref.shape, F32)

    yb = yt_ref[0]
    lo = lo_ref[idx]
    hi = hi_ref[idx]

    def body(i, _):
        c0 = lo + 2 * i
        chunks = ((c0, r * EC_SLOT_BLOCK),
                  (jnp.minimum(c0 + 1, hi), jnp.where(c0 + 1 <= hi, r * EC_SLOT_BLOCK, no_slot)))
        adds = []
        for c, first_slot in chunks:
            off = pl.multiple_of(c * tc, tc)
            adds.append((off, jnp.dot(yb, _one_hot(pos_ref, off, tc, first_slot), preferred_element_type=F32)))
        for off, contrib in adds:
            o_ref[0, :, pl.ds(off, tc)] += contrib
        return 0

    lax.fori_loop(0, (hi - lo + 2) // 2, body, 0)


def _combine_call(lo, hi, pos_rows, yt, b, t, cap):
    n_exp, d, _ = yt.shape
    tc = min(EC_TOKEN_CHUNK, t)
    nr = cap // EC_SLOT_BLOCK
    return pl.pallas_call(
        functools.partial(_combine_kernel, tc=tc, no_slot=cap),
        grid_spec=pltpu.PrefetchScalarGridSpec(
            num_scalar_prefetch=2,
            grid=(b, n_exp, nr),
            in_specs=[
                pl.BlockSpec((1, 1, t), lambda bi, e, r, lo_r, hi_r: (bi * n_exp + e, 0, 0)),
                pl.BlockSpec((1, d, EC_SLOT_BLOCK), lambda bi, e, r, lo_r, hi_r: (e, 0, bi * nr + r)),
            ],
            out_specs=pl.BlockSpec((1, d, t), lambda bi, e, r, lo_r, hi_r: (bi, 0, 0)),
        ),
        out_shape=SDS((b, d, t), F32),
        compiler_params=_params("parallel", "arbitrary", "arbitrary"),
        name="ec_combine",
    )(lo, hi, pos_rows, yt)


def _ffn_residual_kernel(acc_ref, x_ref, mod_ref, g_ref, o_ref):
    y = acc_ref[0].T
    g_f = mod_ref[0, 5:6, :]
    o_ref[0] = x_ref[0] + g_f * (_rms(y) * g_ref[...])


def _ffn_residual_call(acc_t, x, mod_l, gpost):
    b, t, d = x.shape
    tt = min(RESID_ROWS, t)
    return pl.pallas_call(
        _ffn_residual_kernel,
        grid=(b, t // tt),
        in_specs=[
            pl.BlockSpec((1, d, tt), lambda bi, i: (bi, 0, i)),
            pl.BlockSpec((1, tt, d), lambda bi, i: (bi, i, 0)),
            pl.BlockSpec((1, 6, d), lambda bi, i: (bi, 0, 0)),
            pl.BlockSpec((1, d), lambda bi, i: (0, 0)),
        ],
        out_specs=pl.BlockSpec((1, tt, d), lambda bi, i: (bi, i, 0)),
        out_shape=SDS((b, t, d), F32),
        compiler_params=_params("parallel", "parallel"),
        name="ffn_residual",
    )(acc_t, x, mod_l, gpost)


def _rope_tables(seq):
    n_rows = seq // GRID_W
    row = jnp.repeat(jnp.arange(n_rows), GRID_W).astype(F32)
    col = jnp.tile(jnp.arange(GRID_W), n_rows).astype(F32)
    inv = 1.0 / (ROPE_THETA ** (jnp.arange(0, ROPE_HALF, 2, dtype=F32) / ROPE_HALF))
    ar = row[:, None] * inv[None]
    ac = col[:, None] * inv[None]
    ang = jnp.concatenate([ar, ar, ac, ac], axis=-1)
    cos, sin = jnp.cos(ang), jnp.sin(ang)
    first_half = (jnp.arange(HEAD_DIM) % ROPE_HALF) < (ROPE_HALF // 2)
    sin_signed = jnp.where(first_half[None], -sin, sin)
    return jnp.tile(cos, (1, 2)), jnp.tile(sin_signed, (1, 2))


def _filter_features(seq):
    pos = jnp.abs(jnp.arange(2 * seq) - seq).astype(F32)
    t = pos / (seq - 1)
    f = jnp.linspace(1e-4, HY_BANDS - 1, HY_BANDS, dtype=F32)
    ang = (2.0 * math.pi * pos / seq)[:, None] * f[None]
    z = jnp.concatenate([t[:, None], jnp.cos(ang), -jnp.sin(ang)], axis=-1)
    return jnp.pad(z, ((0, 0), (0, LANES - HY_EMB)))


def kernel(x, c, ada_w, ada_b, mix_pre_g, mix_post_g, w_in, gla_w_dec, gla_b_dec, gla_norm_g, q_norm_g, k_norm_g, hy_conv_w, hy_conv_b, hy_w1, hy_b1, hy_w2, hy_b2, hy_w3, hy_freq, hy_skip, w_out, ffn_pre_g, ffn_post_g, router_w, exp_w_gate, exp_w_up, exp_w_down):
    b, seq, d = x.shape
    depth = ada_w.shape[0]
    n_exp = router_w.shape[-1]
    cap = EC_FACTOR * seq // n_exp
    hw = hy_skip.shape[-1]
    nb = seq // HY_BLOCK
    assert seq % HY_BLOCK == 0 and (nb * b) % 8 == 0

    mod = _ada_call(c, ada_w, ada_b).reshape(depth, b, 6, d)
    cos2, sin2 = _rope_tables(seq)
    feat = _filter_features(seq)
    deltas = jnp.abs(jnp.linspace(HY_MIN_DECAY, HY_MAX_DECAY, hw, dtype=F32))[None]

    for l in range(depth):
        w = w_in[l]
        w_pack = jnp.concatenate(
            [w[:, :768], jnp.pad(w[:, 768:800], ((0, 0), (0, LANES - 2 * GLA_LOWRANK))), w[:, 800:]], axis=1
        ).astype(BF16)
        gqk, gv, gg, glr, aq, ak, av, hy = _inproj_call(
            x, mod[l], mix_pre_g[l][None], w_pack, cos2, sin2,
            jnp.tile(q_norm_g[l], 2)[None], jnp.tile(k_norm_g[l], 2)[None])
        o_att = _attn_call(aq, ak, av)

        wdec_pad = jnp.zeros((2, LANES, GLA_QK), F32)
        wdec_pad = wdec_pad.at[0, :GLA_LOWRANK].set(gla_w_dec[l, 0]).at[1, GLA_LOWRANK:2 * GLA_LOWRANK].set(gla_w_dec[l, 1])
        o_gla = _gla_call(gqk, gv, gg, glr, wdec_pad, gla_b_dec[l][:, None, :],
                          jnp.tile(gla_norm_g[l], GLA_HEADS)[None])

        z, x0 = _hy_pre_call(hy, hy_conv_w[l], hy_conv_b[l][None])
        w1p = jnp.pad(hy_w1[l], ((0, LANES - HY_EMB), (0, 0)))
        filt = _hy_filter_call(feat, w1p, hy_b1[l][None], hy_w2[l], hy_b2[l][:, None, :], hy_w3[l],
                               hy_freq[l][None], deltas, seq)
        ft = filt.T.reshape(hw, 2 * nb, HY_BLOCK)
        zt = jnp.transpose(z.reshape(b, nb, HY_BLOCK, hw), (3, 1, 0, 2)).reshape(hw, nb * b, HY_BLOCK)
        yt = _hy_conv_call(zt, ft, nb, b).reshape(hw, nb, b, HY_BLOCK)
        yc = jnp.transpose(yt, (2, 1, 3, 0)).reshape(b, seq, hw)

        rw_full = jnp.pad(router_w[l], ((0, 0), (0, LANES - n_exp)))
        rw_hi = rw_full.astype(BF16)
        rw_pad = jnp.stack([rw_hi, (rw_full - rw_hi.astype(F32)).astype(BF16)])
        x, h2, aff = _outproj_call(x, o_gla, o_att, yc, z, x0, hy_skip[l][None], w_out[l].astype(BF16), mod[l],
                                   mix_post_g[l][None], ffn_pre_g[l][None], rw_pad, n_exp)

        aff_t = jnp.swapaxes(aff[..., :n_exp], 1, 2)
        pos_rows, cum = _route_call(aff_t, cap)
        lo, hi = _chunk_ranges(cum, cap, min(EC_TOKEN_CHUNK, seq))
        aff_rows = aff_t.reshape(b * n_exp, 1, seq)
        xgt, gc = _dispatch_call(lo, hi, pos_rows, aff_rows, h2, n_exp, cap)
        yt = _ffn_call(xgt, gc, exp_w_gate, exp_w_up, exp_w_down, l)
        acc_t = _combine_call(lo, hi, pos_rows, yt, b, seq, cap)
        x = _ffn_residual_call(acc_t, x, mod[l], ffn_post_g[l][None])
    return x
```
